```python
import jax, jax.numpy as jnp
from jax import lax
import numpy as np

D_MODEL = 1024
BATCH = 8
SEQ = 2048
DEPTH = 4
DEC_BATCH = 8
DEC_SEQ = 64
PAST_LEN = 1024

CHUNK = 64
D_MIX = D_MODEL
D_POOL = D_MIX // 2
D_CONV = D_MIX - D_POOL
POOL_WINDOWS = (2, 4, 8, 16)
N_POOL_GROUPS = len(POOL_WINDOWS)
POOL_GROUP = D_POOL // N_POOL_GROUPS
POOL_STATE = max(POOL_WINDOWS) - 1
N_CONV_HEADS = 8
CONV_HEAD = D_CONV // N_CONV_HEADS
CONV_WIDTH = 3
D_FF = ((8 * D_MODEL // 3 + 127) // 128) * 128
D_IN = D_POOL + 3 * D_CONV
EPS = 1e-6

kernel_name = "hybrid_pool_shortconv_streaming_encoder_step"


def rmsnorm(x, g):
    xf = x.astype(jnp.float32)
    y = xf * lax.rsqrt(jnp.mean(xf * xf, axis=-1, keepdims=True) + EPS)
    return (y * g.astype(jnp.float32)).astype(x.dtype)


def head_rmsnorm(x, n_heads, g):
    b, t, c = x.shape
    xf = x.astype(jnp.float32).reshape(b, t, n_heads, c // n_heads)
    y = xf * lax.rsqrt(jnp.mean(xf * xf, axis=-1, keepdims=True) + EPS)
    return (y.reshape(b, t, c) * g.astype(jnp.float32)).astype(x.dtype)


def causal_dwconv(x, buf, w):
    t = x.shape[1]
    xp = jnp.concatenate([buf.astype(x.dtype), x], axis=1)
    y = sum(w[k] * xp[:, k:k + t] for k in range(CONV_WIDTH))
    return y, xp[:, -(CONV_WIDTH - 1):]


def multiscale_pool(v, buf, offset):
    b, t, c = v.shape
    L = POOL_STATE
    xp = jnp.concatenate([buf.astype(v.dtype), v], axis=1)
    cs = jnp.cumsum(xp.astype(jnp.float32), axis=1)
    cs0 = jnp.concatenate([jnp.zeros((b, 1, c), jnp.float32), cs], axis=1)
    pos = offset + jnp.arange(t)
    means = []
    for gi, w in enumerate(POOL_WINDOWS):
        sl = slice(gi * POOL_GROUP, (gi + 1) * POOL_GROUP)
        s = cs0[:, L + 1:L + 1 + t, sl] - cs0[:, L + 1 - w:L + 1 - w + t, sl]
        cnt = jnp.minimum(pos + 1, w).astype(jnp.float32)[None, :, None]
        means.append(s / cnt)
    mean = jnp.concatenate(means, axis=-1)
    out = (mean - v.astype(jnp.float32)).astype(v.dtype)
    return out, xp[:, -L:]


def layer(x, pool_buf, conv_buf, ffn_buf, offset, w_in, pool_mix, pool_scale, conv_w,
          g_pool_out, g_conv_out, w_out, g_pre_mix, g_post_mix, g_pre_ffn, g_post_ffn,
          w_up, ffn_conv_w, w_down):
    b, t, _ = x.shape
    h = rmsnorm(x, g_pre_mix)
    z = h @ w_in
    v = z[..., :D_POOL]
    gb = z[..., D_POOL:D_POOL + D_CONV]
    gc = z[..., D_POOL + D_CONV:D_POOL + 2 * D_CONV]
    u = z[..., D_POOL + 2 * D_CONV:]
    pooled, new_pool = multiscale_pool(v, pool_buf, offset)
    ya = jnp.einsum('btgc,gcd->btgd', pooled.reshape(b, t, N_POOL_GROUPS, POOL_GROUP),
                    pool_mix).reshape(b, t, D_POOL) * pool_scale
    conv_out, new_conv = causal_dwconv(gc * u, conv_buf, conv_w)
    yb = gb * conv_out
    mix = jnp.concatenate([head_rmsnorm(ya, N_POOL_GROUPS, g_pool_out),
                           head_rmsnorm(yb, N_CONV_HEADS, g_conv_out)], axis=-1) @ w_out
    x = x + rmsnorm(mix, g_post_mix)
    h = rmsnorm(x, g_pre_ffn)
    up, new_ffn = causal_dwconv(h @ w_up, ffn_buf, ffn_conv_w)
    f = (jax.nn.silu(up[..., :D_FF]) * up[..., D_FF:]) @ w_down
    x = x + rmsnorm(f, g_post_ffn)
    return x, new_pool, new_conv, new_ffn


def setup_inputs(seed: int = 0) -> dict:
    key = jax.random.key(seed)
    ks = jax.random.split(key, 24)
    f32 = jnp.float32

    def nrm(k, shape, scale):
        return jax.random.normal(k, shape, f32) * scale

    def gain(k, shape):
        return 1.0 + 0.02 * jax.random.normal(k, shape, f32)

    return {
        "x_prompt": nrm(ks[0], (BATCH, SEQ, D_MODEL), 1.0),
        "x_sample": nrm(ks[1], (DEC_BATCH, DEC_SEQ, D_MODEL), 1.0),
        "state_pool": nrm(ks[2], (DEPTH, DEC_BATCH, POOL_STATE, D_POOL), 1.0),
        "state_conv": nrm(ks[3], (DEPTH, DEC_BATCH, CONV_WIDTH - 1, D_CONV), 1.0),
        "state_ffn_conv": nrm(ks[4], (DEPTH, DEC_BATCH, CONV_WIDTH - 1, 2 * D_FF), 1.0),
        "w_in": nrm(ks[5], (DEPTH, D_MODEL, D_IN), D_MODEL ** -0.5),
        "pool_mix": nrm(ks[6], (DEPTH, N_POOL_GROUPS, POOL_GROUP, POOL_GROUP), POOL_GROUP ** -0.5),
        "pool_scale": gain(ks[7], (DEPTH, D_POOL)),
        "conv_w": nrm(ks[8], (DEPTH, CONV_WIDTH, D_CONV), CONV_WIDTH ** -0.5),
        "g_pool_out": gain(ks[9], (DEPTH, D_POOL)),
        "g_conv_out": gain(ks[10], (DEPTH, D_CONV)),
        "w_out": nrm(ks[11], (DEPTH, D_MIX, D_MODEL), D_MIX ** -0.5),
        "g_pre_mix": gain(ks[12], (DEPTH, D_MODEL)),
        "g_post_mix": gain(ks[13], (DEPTH, D_MODEL)),
        "g_pre_ffn": gain(ks[14], (DEPTH, D_MODEL)),
        "g_post_ffn": gain(ks[15], (DEPTH, D_MODEL)),
        "w_up": nrm(ks[16], (DEPTH, D_MODEL, 2 * D_FF), D_MODEL ** -0.5),
        "ffn_conv_w": nrm(ks[17], (DEPTH, CONV_WIDTH, 2 * D_FF), CONV_WIDTH ** -0.5),
        "w_down": nrm(ks[18], (DEPTH, D_FF, D_MODEL), D_FF ** -0.5),
        "g_final": gain(ks[19], (D_MODEL,)),
    }


def reference(x_prompt, x_sample, state_pool, state_conv, state_ffn_conv, w_in, pool_mix,
              pool_scale, conv_w, g_pool_out, g_conv_out, w_out, g_pre_mix, g_post_mix,
              g_pre_ffn, g_post_ffn, w_up, ffn_conv_w, w_down, g_final):
    bp = x_prompt.shape[0]
    dt = x_prompt.dtype
    xp, xs = x_prompt, x_sample
    pool_p, conv_p, ffn_p, pool_s, conv_s, ffn_s = [], [], [], [], [], []
    for l in range(DEPTH):
        weights = (w_in[l], pool_mix[l], pool_scale[l], conv_w[l], g_pool_out[l], g_conv_out[l],
                   w_out[l], g_pre_mix[l], g_post_mix[l], g_pre_ffn[l], g_post_ffn[l],
                   w_up[l], ffn_conv_w[l], w_down[l])
        xp, a, c, f = layer(xp,
                            jnp.zeros((bp, POOL_STATE, D_POOL), dt),
                            jnp.zeros((bp, CONV_WIDTH - 1, D_CONV), dt),
                            jnp.zeros((bp, CONV_WIDTH - 1, 2 * D_FF), dt),
                            0, *weights)
        pool_p.append(a); conv_p.append(c); ffn_p.append(f)
        xs, a, c, f = layer(xs, state_pool[l], state_conv[l], state_ffn_conv[l],
                            PAST_LEN, *weights)
        pool_s.append(a); conv_s.append(c); ffn_s.append(f)
    y_prompt = rmsnorm(xp, g_final)
    y_sample = rmsnorm(xs, g_final)
    return (y_prompt, y_sample,
            jnp.stack(pool_p), jnp.stack(conv_p), jnp.stack(ffn_p),
            jnp.stack(pool_s), jnp.stack(conv_s), jnp.stack(ffn_s))
```

```python
import functools

import jax
import jax.numpy as jnp
from jax import lax
from jax.experimental import pallas as pl
from jax.experimental.pallas import tpu as pltpu

D_MODEL = 1024
D_POOL = 512
D_CONV = 512
POOL_WINDOWS = (2, 4, 8, 16)
POOL_GROUP = 128
POOL_STATE = 15
CONV_HEAD = 64
CONV_WIDTH = 3
D_FF = 2816
D_IN = 2048
PAST_LEN = 1024
EPS = 1e-6

SUBLANES = 8
LANES = 128
POOL_HDR = 16
CONV_HDR = 8
FF_CHUNK = 256
N_FF_CHUNKS = D_FF // FF_CHUNK
SEG_BLOCK = 256
VMEM_LIMIT_BYTES = 56 * 1024 * 1024

BF16 = jnp.bfloat16
F32 = jnp.float32


def _rms(x, g):
    ms = jnp.mean(x * x, axis=-1, keepdims=True)
    return x * lax.rsqrt(ms + EPS) * g


def _layer_kernel(x_ref, ps_ref, cs_ref, fs_ref, w_in_ref, pmix_ref, pscale_ref,
                  convw_ref, gpo_ref, gco_ref, w_out_ref, g1_ref, g2_ref, g3_ref,
                  g4_ref, w_up_ref, fcw_ref, w_down_ref,
                  xo_ref, npool_ref, nconv_ref, nffn_ref,
                  vbuf, cbuf, ubuf, hbuf, h2buf, acc_ref, *, nb, st, offset):
    t = pl.program_id(1)
    rows = nb * st

    @pl.when(t == 0)
    def _():
        vbuf[:, 0:POOL_HDR, :] = ps_ref[...]
        cbuf[:, 0:CONV_HDR, :] = cs_ref[...]
        hbuf[...] = fs_ref[...]

    x = x_ref[...].reshape(rows, D_MODEL)
    h = _rms(x, g1_ref[...]).astype(BF16)
    z = jnp.dot(h, w_in_ref[...], preferred_element_type=F32)
    vbuf[:, POOL_HDR:POOL_HDR + st, :] = z[:, :D_POOL].reshape(nb, st, D_POOL)
    gb = z[:, D_POOL:D_POOL + D_CONV]
    cu = z[:, D_POOL + D_CONV:D_POOL + 2 * D_CONV] * z[:, D_POOL + 2 * D_CONV:]
    cbuf[:, CONV_HDR:CONV_HDR + st, :] = cu.reshape(nb, st, D_CONV)

    pos = offset + t * st + lax.broadcasted_iota(jnp.int32, (st, POOL_GROUP), 0)
    pieces = []
    for gi, w in enumerate(POOL_WINDOWS):
        cols = slice(gi * POOL_GROUP, (gi + 1) * POOL_GROUP)
        cur = vbuf[:, POOL_HDR:POOL_HDR + st, cols]
        s = cur
        for k in range(1, w):
            s = s + vbuf[:, POOL_HDR - k:POOL_HDR - k + st, cols]
        cnt = jnp.minimum(pos + 1, w).astype(F32)
        pooled = (s / cnt[None] - cur).reshape(rows, POOL_GROUP)
        ya = jnp.dot(pooled.astype(BF16), pmix_ref[gi], preferred_element_type=F32)
        ya = ya * pscale_ref[:, cols]
        pieces.append(_rms(ya, gpo_ref[:, cols]).astype(BF16))

    cw = convw_ref[...]
    conv = cw[CONV_WIDTH - 1:CONV_WIDTH] * cbuf[:, CONV_HDR:CONV_HDR + st, :]
    for k in range(1, CONV_WIDTH):
        conv = conv + (cw[CONV_WIDTH - 1 - k:CONV_WIDTH - k]
                       * cbuf[:, CONV_HDR - k:CONV_HDR - k + st, :])
    yb = gb * conv.reshape(rows, D_CONV)
    sq = (yb * yb).astype(BF16)
    ri = lax.broadcasted_iota(jnp.int32, (SEG_BLOCK, SEG_BLOCK), 0) // CONV_HEAD
    ci = lax.broadcasted_iota(jnp.int32, (SEG_BLOCK, SEG_BLOCK), 1) // CONV_HEAD
    seg = jnp.where(ri == ci, 1.0, 0.0).astype(BF16)
    for j in range(D_CONV // SEG_BLOCK):
        cols = slice(j * SEG_BLOCK, (j + 1) * SEG_BLOCK)
        ss = jnp.dot(sq[:, cols], seg, preferred_element_type=F32)
        ybn = yb[:, cols] * lax.rsqrt(ss * (1.0 / CONV_HEAD) + EPS) * gco_ref[:, cols]
        pieces.append(ybn.astype(BF16))

    cat = jnp.concatenate(pieces, axis=-1)
    mix = jnp.dot(cat, w_out_ref[...], preferred_element_type=F32)
    x1 = x + _rms(mix, g2_ref[...])
    h2buf[...] = _rms(x1, g3_ref[...]).astype(BF16)

    pool_tail = vbuf[:, st:st + POOL_HDR, :]
    conv_tail = cbuf[:, st:st + CONV_HDR, :]
    npool_ref[...] = pool_tail
    nconv_ref[...] = conv_tail
    vbuf[:, 0:POOL_HDR, :] = pool_tail
    cbuf[:, 0:CONV_HDR, :] = conv_tail

    acc_ref[...] = jnp.zeros_like(acc_ref)

    def chunk(c, carry):
        up = jnp.dot(h2buf[...], w_up_ref[c], preferred_element_type=F32)
        ubuf[:, 0:CONV_HDR, :] = hbuf[c]
        ubuf[:, CONV_HDR:CONV_HDR + st, :] = up.reshape(nb, st, 2 * FF_CHUNK)
        fw = fcw_ref[c]
        cv = fw[CONV_WIDTH - 1:CONV_WIDTH] * ubuf[:, CONV_HDR:CONV_HDR + st, :]
        for k in range(1, CONV_WIDTH):
            cv = cv + (fw[CONV_WIDTH - 1 - k:CONV_WIDTH - k]
                       * ubuf[:, CONV_HDR - k:CONV_HDR - k + st, :])
        tail = ubuf[:, st:st + CONV_HDR, :]
        hbuf[c] = tail
        nffn_ref[c] = tail
        cv = cv.reshape(rows, 2 * FF_CHUNK)
        gate = cv[:, :FF_CHUNK]
        act = (gate / (1.0 + jnp.exp(-gate)) * cv[:, FF_CHUNK:]).astype(BF16)
        acc_ref[...] += jnp.dot(act, w_down_ref[c], preferred_element_type=F32)
        return carry

    lax.fori_loop(0, N_FF_CHUNKS, chunk, 0)
    out = x1 + _rms(acc_ref[...], g4_ref[...])
    xo_ref[...] = out.reshape(nb, st, D_MODEL)


def _const_spec(shape):
    zeros = (0,) * len(shape)
    return pl.BlockSpec(shape, lambda b, t: zeros, pipeline_mode=pl.Buffered(1))


def _layer_call(x, pool_state, conv_state, ffn_state, lw, *, nb, st, offset):
    batch, seq, _ = x.shape
    grid = (batch // nb, seq // st)
    rows = nb * st
    kern = functools.partial(_layer_kernel, nb=nb, st=st, offset=offset)
    in_specs = [
        pl.BlockSpec((nb, st, D_MODEL), lambda b, t: (b, t, 0)),
        pl.BlockSpec((nb, POOL_HDR, D_POOL), lambda b, t: (b, 0, 0)),
        pl.BlockSpec((nb, CONV_HDR, D_CONV), lambda b, t: (b, 0, 0)),
        pl.BlockSpec((N_FF_CHUNKS, nb, CONV_HDR, 2 * FF_CHUNK), lambda b, t: (0, b, 0, 0)),
    ] + [_const_spec(w.shape) for w in lw]
    out_shape = (
        jax.ShapeDtypeStruct(x.shape, F32),
        jax.ShapeDtypeStruct((batch, POOL_HDR, D_POOL), F32),
        jax.ShapeDtypeStruct((batch, CONV_HDR, D_CONV), F32),
        jax.ShapeDtypeStruct((N_FF_CHUNKS, batch, CONV_HDR, 2 * FF_CHUNK), F32),
    )
    out_specs = (
        pl.BlockSpec((nb, st, D_MODEL), lambda b, t: (b, t, 0)),
        pl.BlockSpec((nb, POOL_HDR, D_POOL), lambda b, t: (b, 0, 0)),
        pl.BlockSpec((nb, CONV_HDR, D_CONV), lambda b, t: (b, 0, 0)),
        pl.BlockSpec((N_FF_CHUNKS, nb, CONV_HDR, 2 * FF_CHUNK), lambda b, t: (0, b, 0, 0)),
    )
    scratch = [
        pltpu.VMEM((nb, POOL_HDR + st, D_POOL), F32),
        pltpu.VMEM((nb, CONV_HDR + st, D_CONV), F32),
        pltpu.VMEM((nb, CONV_HDR + st, 2 * FF_CHUNK), F32),
        pltpu.VMEM((N_FF_CHUNKS, nb, CONV_HDR, 2 * FF_CHUNK), F32),
        pltpu.VMEM((rows, D_MODEL), BF16),
        pltpu.VMEM((rows, D_MODEL), F32),
    ]
    return pl.pallas_call(
        kern,
        out_shape=out_shape,
        grid=grid,
        in_specs=in_specs,
        out_specs=out_specs,
        scratch_shapes=scratch,
        compiler_params=pltpu.CompilerParams(
            dimension_semantics=("arbitrary", "arbitrary"),
            vmem_limit_bytes=VMEM_LIMIT_BYTES),
        name="encoder_layer",
    )(x, pool_state, conv_state, ffn_state, *lw)


def _final_norm_kernel(x_ref, g_ref, o_ref):
    o_ref[...] = _rms(x_ref[...], g_ref[...])


def _final_norm(x, g, *, rows):
    n = x.shape[0]
    return pl.pallas_call(
        _final_norm_kernel,
        out_shape=jax.ShapeDtypeStruct(x.shape, F32),
        grid=(n // rows,),
        in_specs=[pl.BlockSpec((rows, D_MODEL), lambda i: (i, 0)),
                  pl.BlockSpec((1, D_MODEL), lambda i: (0, 0))],
        out_specs=pl.BlockSpec((rows, D_MODEL), lambda i: (i, 0)),
        name="final_norm",
    )(x, g)


def _chunk_cols(a):
    lead = a.shape[:-1]
    a = a.reshape(lead + (2, N_FF_CHUNKS, FF_CHUNK))
    a = jnp.swapaxes(a, -3, -2)
    return a.reshape(lead + (N_FF_CHUNKS, 2 * FF_CHUNK))


def _unchunk_cols(a):
    lead = a.shape[:-2]
    a = a.reshape(lead + (N_FF_CHUNKS, 2, FF_CHUNK))
    a = jnp.swapaxes(a, -3, -2)
    return a.reshape(lead + (2 * D_FF,))


def _pad_front(a, n):
    pad = [(0, 0)] * a.ndim
    pad[-2] = (n - a.shape[-2], 0)
    return jnp.pad(a, pad)


def _ffn_state_in(s):
    s = _chunk_cols(_pad_front(s, CONV_HDR))
    return jnp.transpose(s, (2, 0, 1, 3))


def _ffn_state_out(s):
    s = jnp.transpose(s, (1, 2, 0, 3))[:, CONV_HDR - (CONV_WIDTH - 1):]
    return _unchunk_cols(s)


def kernel(x_prompt, x_sample, state_pool, state_conv, state_ffn_conv, w_in, pool_mix,
           pool_scale, conv_w, g_pool_out, g_conv_out, w_out, g_pre_mix, g_post_mix,
           g_pre_ffn, g_post_ffn, w_up, ffn_conv_w, w_down, g_final):
    depth = w_in.shape[0]
    bp, sp, _ = x_prompt.shape
    bs, ss, _ = x_sample.shape

    w_in_b = w_in.astype(BF16)
    pmix_b = pool_mix.astype(BF16)
    w_out_b = w_out.astype(BF16)
    w_up_b = jnp.transpose(_chunk_cols(w_up.astype(BF16)), (0, 2, 1, 3))
    fcw = jnp.transpose(_chunk_cols(ffn_conv_w), (0, 2, 1, 3))
    w_down_b = w_down.astype(BF16).reshape(depth, N_FF_CHUNKS, FF_CHUNK, D_MODEL)
    row = lambda a: a[:, None, :]

    def layer_weights(l):
        return (w_in_b[l], pmix_b[l], row(pool_scale)[l], conv_w[l], row(g_pool_out)[l],
                row(g_conv_out)[l], w_out_b[l], row(g_pre_mix)[l], row(g_post_mix)[l],
                row(g_pre_ffn)[l], row(g_post_ffn)[l], w_up_b[l], fcw[l], w_down_b[l])

    zp = jnp.zeros((bp, POOL_HDR, D_POOL), F32)
    zc = jnp.zeros((bp, CONV_HDR, D_CONV), F32)
    zf = jnp.zeros((N_FF_CHUNKS, bp, CONV_HDR, 2 * FF_CHUNK), F32)
    sp_in = _pad_front(state_pool, POOL_HDR)
    sc_in = _pad_front(state_conv, CONV_HDR)

    xp, xs = x_prompt, x_sample
    outs = [[] for _ in range(6)]
    for l in range(depth):
        lw = layer_weights(l)
        xp, a, c, f = _layer_call(xp, zp, zc, zf, lw, nb=1, st=512, offset=0)
        outs[0].append(a[:, POOL_HDR - POOL_STATE:])
        outs[1].append(c[:, CONV_HDR - (CONV_WIDTH - 1):])
        outs[2].append(_ffn_state_out(f))
        xs, a, c, f = _layer_call(xs, sp_in[l], sc_in[l], _ffn_state_in(state_ffn_conv[l]),
                                  lw, nb=bs, st=ss, offset=PAST_LEN)
        outs[3].append(a[:, POOL_HDR - POOL_STATE:])
        outs[4].append(c[:, CONV_HDR - (CONV_WIDTH - 1):])
        outs[5].append(_ffn_state_out(f))

    g = g_final[None, :]
    y_prompt = _final_norm(xp.reshape(bp * sp, D_MODEL), g, rows=512).reshape(xp.shape)
    y_sample = _final_norm(xs.reshape(bs * ss, D_MODEL), g, rows=512).reshape(xs.shape)
    return (y_prompt, y_sample) + tuple(jnp.stack(o) for o in outs)
```

```python
import functools

import jax
import jax.numpy as jnp
from jax import lax
from jax.experimental import pallas as pl
from jax.experimental.pallas import tpu as pltpu

D_MODEL = 1024
D_POOL = 512
D_CONV = 512
POOL_WINDOWS = (2, 4, 8, 16)
POOL_GROUP = 128
POOL_STATE = 15
CONV_HEAD = 64
CONV_WIDTH = 3
D_FF = 2816
D_IN = 2048
PAST_LEN = 1024
EPS = 1e-6

BATCH = 8
TILE_POS = 64
TILE_ROWS = TILE_POS * BATCH
POOL_HIST = POOL_STATE + 1
CONV_HIST = CONV_WIDTH - 1
FF_CHUNK = 256
N_FF_CHUNKS = D_FF // FF_CHUNK
SEG_BLOCK = 256
VMEM_LIMIT_BYTES = 56 * 1024 * 1024

BF16 = jnp.bfloat16
F32 = jnp.float32


def _rms(x, g):
    ms = jnp.mean(x * x, axis=-1, keepdims=True)
    return x * lax.rsqrt(ms + EPS) * g


def _shift(hist, cur, k):
    if k == 0:
        return cur
    n = cur.shape[0]
    return jnp.concatenate([hist[hist.shape[0] - k * BATCH:], cur[:n - k * BATCH]], axis=0)


def _causal_conv(hist, cur, w):
    out = w[CONV_WIDTH - 1:CONV_WIDTH] * cur
    for k in range(1, CONV_WIDTH):
        out = out + w[CONV_WIDTH - 1 - k:CONV_WIDTH - k] * _shift(hist, cur, k)
    return out


def _layer_kernel(x_ref, ps_ref, cs_ref, fs_ref, w_in_ref, pmix_ref, pscale_ref,
                  convw_ref, gpo_ref, gco_ref, w_out_ref, g1_ref, g2_ref, g3_ref,
                  g4_ref, w_up_ref, fcw_ref, w_down_ref, gf_ref,
                  xo_ref, npool_ref, nconv_ref, nffn_ref,
                  vhist, chist, fhist, *, n_prompt_tiles, final):
    t = pl.program_id(0)
    is_sample = t >= n_prompt_tiles

    @pl.when((t == 0) | (t == n_prompt_tiles))
    def _():
        vhist[...] = ps_ref[0]
        chist[...] = cs_ref[0]
        fhist[...] = fs_ref[0]

    x = x_ref[...]
    h = _rms(x, g1_ref[...]).astype(BF16)
    z = jnp.dot(h, w_in_ref[...], preferred_element_type=F32)
    v = z[:, :D_POOL]
    gb = z[:, D_POOL:D_POOL + D_CONV]
    cu = z[:, D_POOL + D_CONV:D_POOL + 2 * D_CONV] * z[:, D_POOL + 2 * D_CONV:]

    base = jnp.where(is_sample, PAST_LEN, t * TILE_POS)
    pos = base + lax.broadcasted_iota(jnp.int32, (TILE_ROWS, POOL_GROUP), 0) // BATCH
    pieces = []
    for gi, w in enumerate(POOL_WINDOWS):
        cols = slice(gi * POOL_GROUP, (gi + 1) * POOL_GROUP)
        cur = v[:, cols]
        s = jnp.concatenate([vhist[:, cols], cur], axis=0)
        span = 1
        while span < w:
            s = s[span * BATCH:] + s[:s.shape[0] - span * BATCH]
            span *= 2
        s = s[s.shape[0] - TILE_ROWS:]
        cnt = jnp.minimum(pos + 1, w).astype(F32)
        pooled = s / cnt - cur
        ya = jnp.dot(pooled.astype(BF16), pmix_ref[gi], preferred_element_type=F32)
        ya = ya * pscale_ref[:, cols]
        pieces.append(_rms(ya, gpo_ref[:, cols]).astype(BF16))
    pool_tail = v[TILE_ROWS - POOL_HIST * BATCH:]
    npool_ref[0] = pool_tail
    vhist[...] = pool_tail

    yb = gb * _causal_conv(chist[...], cu, convw_ref[...])
    conv_tail = cu[TILE_ROWS - CONV_HIST * BATCH:]
    nconv_ref[0] = conv_tail
    chist[...] = conv_tail
    sq = (yb * yb).astype(BF16)
    ri = lax.broadcasted_iota(jnp.int32, (SEG_BLOCK, SEG_BLOCK), 0) // CONV_HEAD
    ci = lax.broadcasted_iota(jnp.int32, (SEG_BLOCK, SEG_BLOCK), 1) // CONV_HEAD
    seg = jnp.where(ri == ci, 1.0, 0.0).astype(BF16)
    for j in range(D_CONV // SEG_BLOCK):
        cols = slice(j * SEG_BLOCK, (j + 1) * SEG_BLOCK)
        ss = jnp.dot(sq[:, cols], seg, preferred_element_type=F32)
        ybn = yb[:, cols] * lax.rsqrt(ss * (1.0 / CONV_HEAD) + EPS) * gco_ref[:, cols]
        pieces.append(ybn.astype(BF16))

    cat = jnp.concatenate(pieces, axis=-1)
    mix = jnp.dot(cat, w_out_ref[...], preferred_element_type=F32)
    x1 = x + _rms(mix, g2_ref[...])
    h2 = _rms(x1, g3_ref[...]).astype(BF16)

    acts = []
    for c in range(N_FF_CHUNKS):
        halves = []
        for lo in (c * FF_CHUNK, D_FF + c * FF_CHUNK):
            cols = slice(lo, lo + FF_CHUNK)
            up = jnp.dot(h2, w_up_ref[:, cols], preferred_element_type=F32)
            halves.append(_causal_conv(fhist[:, cols], up, fcw_ref[:, cols]))
            tail = up[TILE_ROWS - CONV_HIST * BATCH:]
            nffn_ref[0, :, cols] = tail
            fhist[:, cols] = tail
        gate, val = halves
        acts.append((gate / (1.0 + jnp.exp(-gate)) * val).astype(BF16))
    act = jnp.concatenate(acts, axis=-1)
    f = jnp.dot(act, w_down_ref[...], preferred_element_type=F32)
    out = x1 + _rms(f, g4_ref[...])
    if final:
        out = _rms(out, gf_ref[...])
    xo_ref[...] = out


def _const_spec(shape):
    zeros = (0,) * len(shape)
    return pl.BlockSpec(shape, lambda t: zeros, pipeline_mode=pl.Buffered(1))


def _layer_call(x, pool_state, conv_state, ffn_state, lw, *, n_prompt_tiles, final):
    n_tiles = x.shape[0] // TILE_ROWS
    kern = functools.partial(_layer_kernel, n_prompt_tiles=n_prompt_tiles, final=final)

    def group_spec(a):
        return pl.BlockSpec((1,) + a.shape[1:], lambda t: (t // n_prompt_tiles, 0, 0))

    states = (pool_state, conv_state, ffn_state)
    x_spec = pl.BlockSpec((TILE_ROWS, D_MODEL), lambda t: (t, 0))
    return pl.pallas_call(
        kern,
        out_shape=(jax.ShapeDtypeStruct(x.shape, F32),)
        + tuple(jax.ShapeDtypeStruct(s.shape, F32) for s in states),
        grid=(n_tiles,),
        in_specs=[x_spec] + [group_spec(s) for s in states] + [_const_spec(w.shape) for w in lw],
        out_specs=(x_spec,) + tuple(group_spec(s) for s in states),
        scratch_shapes=[pltpu.VMEM(s.shape[1:], F32) for s in states],
        compiler_params=pltpu.CompilerParams(
            dimension_semantics=("arbitrary",),
            vmem_limit_bytes=VMEM_LIMIT_BYTES),
        name="encoder_layer",
    )(x, *states, *lw)


def _time_major(a):
    b, t, c = a.shape
    return jnp.swapaxes(a, 0, 1).reshape(t * b, c)


def _batch_major(a, hist):
    a = a.reshape(a.shape[:-2] + (hist, BATCH, a.shape[-1]))
    return jnp.swapaxes(a, -3, -2)


def _state_in(s, hist):
    s = jnp.swapaxes(s, 1, 2)
    s = jnp.pad(s, ((0, 0), (hist - s.shape[1], 0), (0, 0), (0, 0)))
    s = s.reshape(s.shape[0], hist * BATCH, s.shape[-1])
    return jnp.stack([jnp.zeros_like(s), s], axis=1)


def kernel(x_prompt, x_sample, state_pool, state_conv, state_ffn_conv, w_in, pool_mix,
           pool_scale, conv_w, g_pool_out, g_conv_out, w_out, g_pre_mix, g_post_mix,
           g_pre_ffn, g_post_ffn, w_up, ffn_conv_w, w_down, g_final):
    depth = w_in.shape[0]
    bp, sp, _ = x_prompt.shape
    bs, ss, _ = x_sample.shape
    assert bp == BATCH and bs == BATCH and sp % TILE_POS == 0 and ss == TILE_POS
    n_prompt_tiles = sp // TILE_POS

    row = lambda a: a[:, None, :]
    weights = (w_in.astype(BF16), pool_mix.astype(BF16), row(pool_scale), conv_w,
               row(g_pool_out), row(g_conv_out), w_out.astype(BF16), row(g_pre_mix),
               row(g_post_mix), row(g_pre_ffn), row(g_post_ffn), w_up.astype(BF16),
               ffn_conv_w, w_down.astype(BF16))
    gf = g_final[None, :]

    pool_in = _state_in(state_pool, POOL_HIST)
    conv_in = _state_in(state_conv, CONV_HIST)
    ffn_in = _state_in(state_ffn_conv, CONV_HIST)

    x = jnp.concatenate([_time_major(x_prompt), _time_major(x_sample)], axis=0)
    pools, convs, ffns = [], [], []
    for l in range(depth):
        lw = tuple(w[l] for w in weights) + (gf,)
        x, a, c, f = _layer_call(x, pool_in[l], conv_in[l], ffn_in[l], lw,
                                 n_prompt_tiles=n_prompt_tiles, final=(l == depth - 1))
        pools.append(a)
        convs.append(c)
        ffns.append(f)

    n_p = sp * BATCH
    y_prompt = jnp.swapaxes(x[:n_p].reshape(sp, BATCH, D_MODEL), 0, 1)
    y_sample = jnp.swapaxes(x[n_p:].reshape(ss, BATCH, D_MODEL), 0, 1)
    pools = _batch_major(jnp.stack(pools), POOL_HIST)[..., POOL_HIST - POOL_STATE:, :]
    convs = _batch_major(jnp.stack(convs), CONV_HIST)
    ffns = _batch_major(jnp.stack(ffns), CONV_HIST)
    return (y_prompt, y_sample, pools[:, 0], convs[:, 0], ffns[:, 0],
            pools[:, 1], convs[:, 1], ffns[:, 1])
```

```python
import functools

import jax
import jax.numpy as jnp
from jax import lax
from jax.experimental import pallas as pl
from jax.experimental.pallas import tpu as pltpu

D_MODEL = 1024
D_POOL = 512
D_CONV = 512
POOL_WINDOWS = (2, 4, 8, 16)
POOL_GROUP = 128
POOL_STATE = 15
CONV_HEAD = 64
CONV_WIDTH = 3
D_FF = 2816
D_IN = 2048
PAST_LEN = 1024
EPS = 1e-6

BATCH = 8
TILE_POS = 64
TILE_ROWS = TILE_POS * BATCH
POOL_HIST = POOL_STATE + 1
CONV_HIST = CONV_WIDTH - 1
FF_CHUNK = 256
N_FF_CHUNKS = D_FF // FF_CHUNK
SEG_BLOCK = 256
N_SLOTS = 2
VMEM_LIMIT_BYTES = 56 * 1024 * 1024

BF16 = jnp.bfloat16
F32 = jnp.float32


def _rms(x, g):
    ms = jnp.mean(x * x, axis=-1, keepdims=True)
    return x * lax.rsqrt(ms + EPS) * g


def _shift(hist, cur, k):
    if k == 0:
        return cur
    n = cur.shape[0]
    return jnp.concatenate([hist[hist.shape[0] - k * BATCH:], cur[:n - k * BATCH]], axis=0)


def _causal_conv(hist, cur, w):
    out = w[CONV_WIDTH - 1:CONV_WIDTH] * cur
    for k in range(1, CONV_WIDTH):
        out = out + w[CONV_WIDTH - 1 - k:CONV_WIDTH - k] * _shift(hist, cur, k)
    return out


def _tile_copies(prompt_hbm, sample_hbm, buf, sem, tile, slot, *, sample, to_hbm):
    copies = []
    for b in range(BATCH):
        hbm = sample_hbm.at[b] if sample else prompt_hbm.at[b, pl.ds(tile * TILE_POS, TILE_POS), :]
        vmem = buf.at[slot, :, b, :]
        src, dst = (vmem, hbm) if to_hbm else (hbm, vmem)
        copies.append(pltpu.make_async_copy(src, dst, sem.at[slot]))
    return copies


def _layer_kernel(xp_hbm, xs_hbm, ps_ref, cs_ref, fs_ref, w_in_ref, pmix_ref, pscale_ref,
                  convw_ref, gpo_ref, gco_ref, w_out_ref, g1_ref, g2_ref, g3_ref,
                  g4_ref, w_up_ref, fcw_ref, w_down_ref, gf_ref,
                  yp_hbm, ys_hbm, npool_ref, nconv_ref, nffn_ref,
                  vhist, chist, fhist, xbuf, obuf, in_sem, out_sem, *, n_prompt_tiles, final):
    t = pl.program_id(0)
    last = n_prompt_tiles
    is_sample = t == last
    slot = t % N_SLOTS

    load = functools.partial(_tile_copies, xp_hbm, xs_hbm, xbuf, in_sem, to_hbm=False)
    store = functools.partial(_tile_copies, yp_hbm, ys_hbm, obuf, out_sem, to_hbm=True)

    def start(copies):
        for cp in copies:
            cp.start()

    def wait(copies):
        for cp in copies:
            cp.wait()

    @pl.when(t == 0)
    def _():
        start(load(0, 0, sample=False))

    @pl.when(t + 1 < last)
    def _():
        start(load(t + 1, 1 - slot, sample=False))

    @pl.when(t + 1 == last)
    def _():
        start(load(0, 1 - slot, sample=True))

    @pl.when(t < last)
    def _():
        wait(load(t, slot, sample=False))

    @pl.when(is_sample)
    def _():
        wait(load(0, slot, sample=True))

    @pl.when((t == 0) | is_sample)
    def _():
        vhist[...] = ps_ref[0]
        chist[...] = cs_ref[0]
        fhist[...] = fs_ref[0]

    x = xbuf[slot].reshape(TILE_ROWS, D_MODEL)
    h = _rms(x, g1_ref[...]).astype(BF16)
    z = jnp.dot(h, w_in_ref[...], preferred_element_type=F32)
    v = z[:, :D_POOL]
    gb = z[:, D_POOL:D_POOL + D_CONV]
    cu = z[:, D_POOL + D_CONV:D_POOL + 2 * D_CONV] * z[:, D_POOL + 2 * D_CONV:]

    base = jnp.where(is_sample, PAST_LEN, t * TILE_POS)
    pos = base + lax.broadcasted_iota(jnp.int32, (TILE_ROWS, POOL_GROUP), 0) // BATCH
    pieces = []
    for gi, w in enumerate(POOL_WINDOWS):
        cols = slice(gi * POOL_GROUP, (gi + 1) * POOL_GROUP)
        cur = v[:, cols]
        s = jnp.concatenate([vhist[:, cols], cur], axis=0)
        span = 1
        while span < w:
            s = s[span * BATCH:] + s[:s.shape[0] - span * BATCH]
            span *= 2
        s = s[s.shape[0] - TILE_ROWS:]
        cnt = jnp.minimum(pos + 1, w).astype(F32)
        pooled = s / cnt - cur
        ya = jnp.dot(pooled.astype(BF16), pmix_ref[gi], preferred_element_type=F32)
        ya = ya * pscale_ref[:, cols]
        pieces.append(_rms(ya, gpo_ref[:, cols]).astype(BF16))
    pool_tail = v[TILE_ROWS - POOL_HIST * BATCH:]
    npool_ref[0] = pool_tail
    vhist[...] = pool_tail

    yb = gb * _causal_conv(chist[...], cu, convw_ref[...])
    conv_tail = cu[TILE_ROWS - CONV_HIST * BATCH:]
    nconv_ref[0] = conv_tail
    chist[...] = conv_tail
    sq = (yb * yb).astype(BF16)
    ri = lax.broadcasted_iota(jnp.int32, (SEG_BLOCK, SEG_BLOCK), 0) // CONV_HEAD
    ci = lax.broadcasted_iota(jnp.int32, (SEG_BLOCK, SEG_BLOCK), 1) // CONV_HEAD
    seg = jnp.where(ri == ci, 1.0, 0.0).astype(BF16)
    for j in range(D_CONV // SEG_BLOCK):
        cols = slice(j * SEG_BLOCK, (j + 1) * SEG_BLOCK)
        ss = jnp.dot(sq[:, cols], seg, preferred_element_type=F32)
        ybn = yb[:, cols] * lax.rsqrt(ss * (1.0 / CONV_HEAD) + EPS) * gco_ref[:, cols]
        pieces.append(ybn.astype(BF16))

    cat = jnp.concatenate(pieces, axis=-1)
    mix = jnp.dot(cat, w_out_ref[...], preferred_element_type=F32)
    x1 = x + _rms(mix, g2_ref[...])
    h2 = _rms(x1, g3_ref[...]).astype(BF16)

    acts = []
    for c in range(N_FF_CHUNKS):
        halves = []
        for lo in (c * FF_CHUNK, D_FF + c * FF_CHUNK):
            cols = slice(lo, lo + FF_CHUNK)
            up = jnp.dot(h2, w_up_ref[:, cols], preferred_element_type=F32)
            halves.append(_causal_conv(fhist[:, cols], up, fcw_ref[:, cols]))
            tail = up[TILE_ROWS - CONV_HIST * BATCH:]
            nffn_ref[0, :, cols] = tail
            fhist[:, cols] = tail
        gate, val = halves
        acts.append((gate / (1.0 + jnp.exp(-gate)) * val).astype(BF16))
    act = jnp.concatenate(acts, axis=-1)
    f = jnp.dot(act, w_down_ref[...], preferred_element_type=F32)
    out = x1 + _rms(f, g4_ref[...])
    if final:
        out = _rms(out, gf_ref[...])

    @pl.when(t >= N_SLOTS)
    def _():
        wait(store(t - N_SLOTS, slot, sample=False))

    obuf[slot] = out.reshape(TILE_POS, BATCH, D_MODEL)

    @pl.when(t < last)
    def _():
        start(store(t, slot, sample=False))

    @pl.when(is_sample)
    def _():
        start(store(0, slot, sample=True))
        wait(store(t - 1, 1 - slot, sample=False))
        wait(store(0, slot, sample=True))


def _layer_call(xp, xs, states, weights, layer, *, final):
    n_prompt_tiles = xp.shape[1] // TILE_POS
    assert n_prompt_tiles >= N_SLOTS
    kern = functools.partial(_layer_kernel, n_prompt_tiles=n_prompt_tiles, final=final)

    def state_spec(a):
        return pl.BlockSpec((None, 1) + a.shape[2:], lambda t: (layer, t // n_prompt_tiles, 0, 0))

    def out_state_spec(a):
        return pl.BlockSpec((1,) + a.shape[2:], lambda t: (t // n_prompt_tiles, 0, 0))

    def weight_spec(w):
        if w.ndim == 2:
            return pl.BlockSpec(w.shape, lambda t: (0, 0), pipeline_mode=pl.Buffered(1))
        zeros = (0,) * (w.ndim - 1)
        return pl.BlockSpec((None,) + w.shape[1:], lambda t: (layer,) + zeros,
                            pipeline_mode=pl.Buffered(1))

    any_spec = pl.BlockSpec(memory_space=pl.ANY)
    return pl.pallas_call(
        kern,
        out_shape=(jax.ShapeDtypeStruct(xp.shape, F32), jax.ShapeDtypeStruct(xs.shape, F32))
        + tuple(jax.ShapeDtypeStruct(s.shape[1:], F32) for s in states),
        grid=(n_prompt_tiles + 1,),
        in_specs=[any_spec, any_spec] + [state_spec(s) for s in states]
        + [weight_spec(w) for w in weights],
        out_specs=(any_spec, any_spec) + tuple(out_state_spec(s) for s in states),
        scratch_shapes=[pltpu.VMEM(s.shape[2:], F32) for s in states] + [
            pltpu.VMEM((N_SLOTS, TILE_POS, BATCH, D_MODEL), F32),
            pltpu.VMEM((N_SLOTS, TILE_POS, BATCH, D_MODEL), F32),
            pltpu.SemaphoreType.DMA((N_SLOTS,)),
            pltpu.SemaphoreType.DMA((N_SLOTS,)),
        ],
        compiler_params=pltpu.CompilerParams(
            dimension_semantics=("arbitrary",),
            vmem_limit_bytes=VMEM_LIMIT_BYTES),
        name="encoder_layer",
    )(xp, xs, *states, *weights)


def _batch_major(a, hist):
    a = a.reshape(a.shape[:-2] + (hist, BATCH, a.shape[-1]))
    return jnp.swapaxes(a, -3, -2)


def _state_in(s, hist):
    s = jnp.swapaxes(s, 1, 2)
    s = jnp.pad(s, ((0, 0), (hist - s.shape[1], 0), (0, 0), (0, 0)))
    s = s.reshape(s.shape[0], hist * BATCH, s.shape[-1])
    return jnp.stack([jnp.zeros_like(s), s], axis=1)


def kernel(x_prompt, x_sample, state_pool, state_conv, state_ffn_conv, w_in, pool_mix,
           pool_scale, conv_w, g_pool_out, g_conv_out, w_out, g_pre_mix, g_post_mix,
           g_pre_ffn, g_post_ffn, w_up, ffn_conv_w, w_down, g_final):
    depth = w_in.shape[0]
    bp, sp, _ = x_prompt.shape
    bs, ss, _ = x_sample.shape
    assert bp == BATCH and bs == BATCH and sp % TILE_POS == 0 and ss == TILE_POS

    row = lambda a: a[:, None, :]
    weights = (w_in.astype(BF16), pool_mix.astype(BF16), row(pool_scale), conv_w,
               row(g_pool_out), row(g_conv_out), w_out.astype(BF16), row(g_pre_mix),
               row(g_post_mix), row(g_pre_ffn), row(g_post_ffn), w_up.astype(BF16),
               ffn_conv_w, w_down.astype(BF16), g_final[None, :])
    states = (_state_in(state_pool, POOL_HIST), _state_in(state_conv, CONV_HIST),
              _state_in(state_ffn_conv, CONV_HIST))

    xp, xs = x_prompt, x_sample
    pools, convs, ffns = [], [], []
    for l in range(depth):
        xp, xs, a, c, f = _layer_call(xp, xs, states, weights, l, final=(l == depth - 1))
        pools.append(a)
        convs.append(c)
        ffns.append(f)

    pools = _batch_major(jnp.stack(pools), POOL_HIST)[..., POOL_HIST - POOL_STATE:, :]
    convs = _batch_major(jnp.stack(convs), CONV_HIST)
    ffns = _batch_major(jnp.stack(ffns), CONV_HIST)
    return (xp, xs, pools[:, 0], convs[:, 0], ffns[:, 0],
            pools[:, 1], convs[:, 1], ffns[:, 1])
```

```python
import functools

import jax
import jax.numpy as jnp
from jax import lax
from jax.experimental import pallas as pl
from jax.experimental.pallas import tpu as pltpu

D_MODEL = 1024
D_POOL = 512
D_CONV = 512
POOL_WINDOWS = (2, 4, 8, 16)
POOL_GROUP = 128
POOL_STATE = 15
CONV_HEAD = 64
CONV_WIDTH = 3
D_FF = 2816
D_IN = 2048
PAST_LEN = 1024
EPS = 1e-6

BATCH = 8
TILE_POS = 64
TILE_ROWS = TILE_POS * BATCH
POOL_HIST = POOL_STATE + 1
CONV_HIST = CONV_WIDTH - 1
FF_CHUNK = 256
N_FF_CHUNKS = D_FF // FF_CHUNK
SEG_BLOCK = 256
PAIRS = 2
PROMPT, SAMPLE = 0, 1
DOWN_ROW_BLOCKS = 2
OUT_COL_BLOCKS = 2
FFN_PIECES = N_FF_CHUNKS + DOWN_ROW_BLOCKS + 1
MIXER_PIECES = 1 + D_IN // D_POOL + 1 + D_CONV // SEG_BLOCK + len(POOL_WINDOWS) + OUT_COL_BLOCKS + 1
MIXER_SPAN = 0.8
MIXER_CLOSE = 0.9
VMEM_LIMIT_BYTES = 58 * 1024 * 1024

BF16 = jnp.bfloat16
F32 = jnp.float32


def _rms(x, g):
    ms = jnp.mean(x * x, axis=-1, keepdims=True)
    return x * lax.rsqrt(ms + EPS) * g


def _shift(hist, cur, k):
    if k == 0:
        return cur
    n = cur.shape[0]
    return jnp.concatenate([hist[hist.shape[0] - k * BATCH:], cur[:n - k * BATCH]], axis=0)


def _causal_conv(hist, cur, w):
    out = w[CONV_WIDTH - 1:CONV_WIDTH] * cur
    for k in range(1, CONV_WIDTH):
        out = out + w[CONV_WIDTH - 1 - k:CONV_WIDTH - k] * _shift(hist, cur, k)
    return out


def _mixer(x_ref, base, vhist, chist, w, x1_ref, h2_ref, res):
    h = _rms(x_ref[...].reshape(TILE_ROWS, D_MODEL), w["g1"][...]).astype(BF16)
    yield
    z = []
    for lo in range(0, D_IN, D_POOL):
        z.append(jnp.dot(h, w["w_in"][:, lo:lo + D_POOL], preferred_element_type=F32))
        yield
    v, gb, gc, u = z
    cu = gc * u
    res["pool_tail"] = v[TILE_ROWS - POOL_HIST * BATCH:]
    res["conv_tail"] = cu[TILE_ROWS - CONV_HIST * BATCH:]

    yb = gb * _causal_conv(chist, cu, w["convw"][...])
    sq = (yb * yb).astype(BF16)
    yield
    ri = lax.broadcasted_iota(jnp.int32, (SEG_BLOCK, SEG_BLOCK), 0) // CONV_HEAD
    ci = lax.broadcasted_iota(jnp.int32, (SEG_BLOCK, SEG_BLOCK), 1) // CONV_HEAD
    seg = jnp.where(ri == ci, 1.0, 0.0).astype(BF16)
    conv_pieces = []
    for j in range(D_CONV // SEG_BLOCK):
        cols = slice(j * SEG_BLOCK, (j + 1) * SEG_BLOCK)
        ss = jnp.dot(sq[:, cols], seg, preferred_element_type=F32)
        ybn = yb[:, cols] * lax.rsqrt(ss * (1.0 / CONV_HEAD) + EPS) * w["gco"][:, cols]
        conv_pieces.append(ybn.astype(BF16))
        yield

    pos = base + lax.broadcasted_iota(jnp.int32, (TILE_ROWS, POOL_GROUP), 0) // BATCH
    pool_pieces = []
    for gi, win in enumerate(POOL_WINDOWS):
        cols = slice(gi * POOL_GROUP, (gi + 1) * POOL_GROUP)
        cur = v[:, cols]
        s = jnp.concatenate([vhist[:, cols], cur], axis=0)
        span = 1
        while span < win:
            s = s[span * BATCH:] + s[:s.shape[0] - span * BATCH]
            span *= 2
        s = s[s.shape[0] - TILE_ROWS:]
        cnt = jnp.minimum(pos + 1, win).astype(F32)
        pooled = s / cnt - cur
        ya = jnp.dot(pooled.astype(BF16), w["pmix"][gi], preferred_element_type=F32)
        ya = ya * w["pscale"][:, cols]
        pool_pieces.append(_rms(ya, w["gpo"][:, cols]).astype(BF16))
        yield

    cat = jnp.concatenate(pool_pieces + conv_pieces, axis=-1)
    block = D_MODEL // OUT_COL_BLOCKS
    mix = []
    for lo in range(0, D_MODEL, block):
        mix.append(jnp.dot(cat, w["w_out"][:, lo:lo + block], preferred_element_type=F32))
        yield
    mix = jnp.concatenate(mix, axis=-1)
    x1 = x_ref[...].reshape(TILE_ROWS, D_MODEL) + _rms(mix, w["g2"][...])
    x1_ref[...] = x1
    h2_ref[...] = _rms(x1, w["g3"][...]).astype(BF16)


def _ffn(x1_ref, h2_ref, fhist, nffn_ref, group, w, final, out_ref):
    acts = []
    for c in range(N_FF_CHUNKS):
        halves = []
        for lo in (c * FF_CHUNK, D_FF + c * FF_CHUNK):
            cols = slice(lo, lo + FF_CHUNK)
            up = jnp.dot(h2_ref[...], w["w_up"][:, cols], preferred_element_type=F32)
            halves.append(_causal_conv(fhist[:, cols], up, w["fcw"][:, cols]))
            tail = up[TILE_ROWS - CONV_HIST * BATCH:]
            nffn_ref[group, :, cols] = tail
            fhist[:, cols] = tail
        gate, val = halves
        acts.append((gate / (1.0 + jnp.exp(-gate)) * val).astype(BF16))
        yield
    act = jnp.concatenate(acts, axis=-1)
    block = TILE_ROWS // DOWN_ROW_BLOCKS
    for lo in range(0, TILE_ROWS, block):
        f = jnp.dot(act[lo:lo + block], w["w_down"][...], preferred_element_type=F32)
        yield
        out = x1_ref[lo:lo + block] + _rms(f, w["g4"][...])
        if final:
            out = _rms(out, w["gf"][...])
        out_ref[lo // BATCH:(lo + block) // BATCH] = out.reshape(block // BATCH, BATCH, D_MODEL)


def _interleave(ffn, n_ffn, mixer, n_mixer):
    order = sorted([(i / n_ffn, 0, ffn) for i in range(n_ffn)]
                   + [(MIXER_SPAN * (k + 0.5) / n_mixer, 1, mixer) for k in range(n_mixer - 1)]
                   + [(MIXER_CLOSE, 1, mixer)],
                   key=lambda e: e[:2])
    for _, _, g in order:
        next(g, None)
    for g in (ffn, mixer):
        assert next(g, "done") == "done"


_WEIGHT_NAMES = ("w_in", "pmix", "pscale", "convw", "gpo", "gco", "w_out", "g1", "g2", "g3",
                 "g4", "w_up", "fcw", "w_down", "gf")


def _layer_kernel(xp_hbm, xs_hbm, ps_ref, cs_ref, fs_ref, *refs, n_prompt_tiles, final):
    nw = len(_WEIGHT_NAMES)
    w = dict(zip(_WEIGHT_NAMES, refs[:nw]))
    (yp_hbm, ys_hbm, npool_ref, nconv_ref, nffn_ref,
     vhist, chist, fhist, xbuf, obuf, x1buf0, x1buf1, h2buf0, h2buf1,
     in_sem, out_sem) = refs[nw:]
    x1buf = (x1buf0, x1buf1)
    h2buf = (h2buf0, h2buf1)
    j = pl.program_id(0)
    n_tiles = n_prompt_tiles + 1
    sample_step = n_prompt_tiles // PAIRS
    on_sample_step = j == sample_step

    def copies(tile, slot, *, sample, to_hbm):
        prompt_hbm, sample_hbm, buf, sem = (
            (yp_hbm, ys_hbm, obuf, out_sem) if to_hbm else (xp_hbm, xs_hbm, xbuf, in_sem))
        out = []
        for b in range(BATCH):
            hbm = sample_hbm.at[b] if sample else prompt_hbm.at[b, pl.ds(tile * TILE_POS, TILE_POS), :]
            vmem = buf.at[slot, :, b, :]
            src, dst = (vmem, hbm) if to_hbm else (hbm, vmem)
            out.append(pltpu.make_async_copy(src, dst, sem.at[slot]))
        return out

    def tile_io(tile, slot, to_hbm, action):
        @pl.when((tile >= 0) & (tile < n_prompt_tiles))
        def _():
            for cp in copies(tile, slot, sample=False, to_hbm=to_hbm):
                getattr(cp, action)()

        @pl.when(tile == n_prompt_tiles)
        def _():
            for cp in copies(0, slot, sample=True, to_hbm=to_hbm):
                getattr(cp, action)()

    load_start = functools.partial(tile_io, to_hbm=False, action="start")
    load_wait = functools.partial(tile_io, to_hbm=False, action="wait")
    store_start = functools.partial(tile_io, to_hbm=True, action="start")
    store_wait = functools.partial(tile_io, to_hbm=True, action="wait")

    mix_tiles = (PAIRS * j, PAIRS * j + 1)
    ffn_tiles = (PAIRS * j - 1, PAIRS * j)

    @pl.when(j == 0)
    def _():
        load_start(mix_tiles[0], 0)
        load_start(mix_tiles[1], 1)
        vhist[...] = ps_ref[PROMPT]
        chist[...] = cs_ref[PROMPT]
        fhist[...] = fs_ref[PROMPT]
        x1buf[1][...] = jnp.zeros((TILE_ROWS, D_MODEL), F32)
        h2buf[1][...] = jnp.zeros((TILE_ROWS, D_MODEL), BF16)

    @pl.when(on_sample_step)
    def _():
        vhist[...] = ps_ref[SAMPLE]
        chist[...] = cs_ref[SAMPLE]

    load_wait(mix_tiles[0], 0)
    store_wait(ffn_tiles[0] - PAIRS, 0)

    for half in range(PAIRS):
        if half == 0:
            base = jnp.where(on_sample_step, PAST_LEN, mix_tiles[0] * TILE_POS)
        else:
            base = mix_tiles[1] * TILE_POS
        src = 1 - half
        if half == 1:
            @pl.when(on_sample_step)
            def _():
                fhist[...] = fs_ref[SAMPLE]
            ffn_group = jnp.where(on_sample_step, SAMPLE, PROMPT)
        else:
            ffn_group = PROMPT
        res = {}
        _interleave(
            _ffn(x1buf[src], h2buf[src], fhist, nffn_ref, ffn_group, w, final,
                 obuf.at[half]), FFN_PIECES,
            _mixer(xbuf.at[half], base, vhist[...], chist[...], w, x1buf[half],
                   h2buf[half], res), MIXER_PIECES)
        pool_tail, conv_tail = res["pool_tail"], res["conv_tail"]
        vhist[...] = pool_tail
        chist[...] = conv_tail
        if half == 0:
            group = jnp.where(on_sample_step, SAMPLE, PROMPT)
            npool_ref[group] = pool_tail
            nconv_ref[group] = conv_tail
        else:
            real = mix_tiles[1] < n_tiles
            npool_ref[PROMPT] = jnp.where(real, pool_tail, npool_ref[PROMPT])
            nconv_ref[PROMPT] = jnp.where(real, conv_tail, nconv_ref[PROMPT])

        store_start(ffn_tiles[half], half)
        load_start(mix_tiles[half] + PAIRS, half)
        if half == 0:
            load_wait(mix_tiles[1], 1)
            store_wait(ffn_tiles[1] - PAIRS, 1)

    @pl.when(j == pl.num_programs(0) - 1)
    def _():
        store_wait(ffn_tiles[0], 0)
        store_wait(ffn_tiles[1], 1)


def _layer_call(xp, xs, states, weights, layer, *, final):
    n_prompt_tiles = xp.shape[1] // TILE_POS
    assert n_prompt_tiles % PAIRS == 0 and n_prompt_tiles >= PAIRS
    n_steps = (n_prompt_tiles + 1 + 1 + PAIRS - 1) // PAIRS
    kern = functools.partial(_layer_kernel, n_prompt_tiles=n_prompt_tiles, final=final)

    def state_spec(a):
        return pl.BlockSpec((None,) + a.shape[1:], lambda j: (layer, 0, 0, 0))

    def out_state_spec(a):
        return pl.BlockSpec(a.shape[1:], lambda j: (0, 0, 0))

    def weight_spec(a):
        if a.ndim == 2:
            return pl.BlockSpec(a.shape, lambda j: (0, 0), pipeline_mode=pl.Buffered(1))
        zeros = (0,) * (a.ndim - 1)
        return pl.BlockSpec((None,) + a.shape[1:], lambda j: (layer,) + zeros,
                            pipeline_mode=pl.Buffered(1))

    any_spec = pl.BlockSpec(memory_space=pl.ANY)
    tile_buf = pltpu.VMEM((PAIRS, TILE_POS, BATCH, D_MODEL), F32)
    return pl.pallas_call(
        kern,
        out_shape=(jax.ShapeDtypeStruct(xp.shape, F32), jax.ShapeDtypeStruct(xs.shape, F32))
        + tuple(jax.ShapeDtypeStruct(s.shape[1:], F32) for s in states),
        grid=(n_steps,),
        in_specs=[any_spec, any_spec] + [state_spec(s) for s in states]
        + [weight_spec(a) for a in weights],
        out_specs=(any_spec, any_spec) + tuple(out_state_spec(s) for s in states),
        scratch_shapes=[pltpu.VMEM(s.shape[2:], F32) for s in states] + [
            tile_buf, tile_buf,
            pltpu.VMEM((TILE_ROWS, D_MODEL), F32), pltpu.VMEM((TILE_ROWS, D_MODEL), F32),
            pltpu.VMEM((TILE_ROWS, D_MODEL), BF16), pltpu.VMEM((TILE_ROWS, D_MODEL), BF16),
            pltpu.SemaphoreType.DMA((PAIRS,)),
            pltpu.SemaphoreType.DMA((PAIRS,)),
        ],
        compiler_params=pltpu.CompilerParams(
            dimension_semantics=("arbitrary",),
            vmem_limit_bytes=VMEM_LIMIT_BYTES),
        name="encoder_layer",
    )(xp, xs, *states, *weights)


def _batch_major(a, hist):
    a = a.reshape(a.shape[:-2] + (hist, BATCH, a.shape[-1]))
    return jnp.swapaxes(a, -3, -2)


def _state_in(s, hist):
    s = jnp.swapaxes(s, 1, 2)
    s = jnp.pad(s, ((0, 0), (hist - s.shape[1], 0), (0, 0), (0, 0)))
    s = s.reshape(s.shape[0], hist * BATCH, s.shape[-1])
    return jnp.stack([jnp.zeros_like(s), s], axis=1)


def kernel(x_prompt, x_sample, state_pool, state_conv, state_ffn_conv, w_in, pool_mix,
           pool_scale, conv_w, g_pool_out, g_conv_out, w_out, g_pre_mix, g_post_mix,
           g_pre_ffn, g_post_ffn, w_up, ffn_conv_w, w_down, g_final):
    depth = w_in.shape[0]
    bp, sp, _ = x_prompt.shape
    bs, ss, _ = x_sample.shape
    assert bp == BATCH and bs == BATCH and sp % TILE_POS == 0 and ss == TILE_POS

    row = lambda a: a[:, None, :]
    weights = dict(
        w_in=w_in.astype(BF16), pmix=pool_mix.astype(BF16), pscale=row(pool_scale), convw=conv_w,
        gpo=row(g_pool_out), gco=row(g_conv_out), w_out=w_out.astype(BF16), g1=row(g_pre_mix),
        g2=row(g_post_mix), g3=row(g_pre_ffn), g4=row(g_post_ffn), w_up=w_up.astype(BF16),
        fcw=ffn_conv_w, w_down=w_down.astype(BF16), gf=g_final[None, :])
    weights = tuple(weights[n] for n in _WEIGHT_NAMES)
    states = (_state_in(state_pool, POOL_HIST), _state_in(state_conv, CONV_HIST),
              _state_in(state_ffn_conv, CONV_HIST))

    xp, xs = x_prompt, x_sample
    pools, convs, ffns = [], [], []
    for l in range(depth):
        xp, xs, a, c, f = _layer_call(xp, xs, states, weights, l, final=(l == depth - 1))
        pools.append(a)
        convs.append(c)
        ffns.append(f)

    pools = _batch_major(jnp.stack(pools), POOL_HIST)[..., POOL_HIST - POOL_STATE:, :]
    convs = _batch_major(jnp.stack(convs), CONV_HIST)
    ffns = _batch_major(jnp.stack(ffns), CONV_HIST)
    return (xp, xs, pools[:, PROMPT], convs[:, PROMPT], ffns[:, PROMPT],
            pools[:, SAMPLE], convs[:, SAMPLE], ffns[:, SAMPLE])
```

```python
import functools

import jax
import jax.numpy as jnp
from jax import lax
from jax.experimental import pallas as pl
from jax.experimental.pallas import tpu as pltpu

D_MODEL = 1024
D_POOL = 512
D_CONV = 512
POOL_WINDOWS = (2, 4, 8, 16)
POOL_GROUP = 128
POOL_STATE = 15
CONV_HEAD = 64
CONV_WIDTH = 3
D_FF = 2816
D_IN = 2048
PAST_LEN = 1024
EPS = 1e-6

BATCH = 8
TILE_POS = 64
TILE_ROWS = TILE_POS * BATCH
POOL_HIST = POOL_STATE + 1
CONV_HIST = CONV_WIDTH - 1
FF_CHUNK = 256
N_FF_CHUNKS = D_FF // FF_CHUNK
SEG_BLOCK = 256
X_SLOTS = 3
OUT_SLOTS = 2
Z_SLOTS = 2
PROMPT, SAMPLE = 0, 1
VMEM_LIMIT_BYTES = 58 * 1024 * 1024

BF16 = jnp.bfloat16
F32 = jnp.float32


def _rms(x, g):
    ms = jnp.mean(x * x, axis=-1, keepdims=True)
    return x * lax.rsqrt(ms + EPS) * g


def _shift(hist, cur, k):
    if k == 0:
        return cur
    n = cur.shape[0]
    return jnp.concatenate([hist[hist.shape[0] - k * BATCH:], cur[:n - k * BATCH]], axis=0)


def _causal_conv(hist, cur, w):
    out = w[CONV_WIDTH - 1:CONV_WIDTH] * cur
    for k in range(1, CONV_WIDTH):
        out = out + w[CONV_WIDTH - 1 - k:CONV_WIDTH - k] * _shift(hist, cur, k)
    return out


def _tile_rows(ref):
    return ref[...].reshape(TILE_ROWS, D_MODEL)


_WEIGHT_NAMES = ("w_in", "pmix", "pscale", "convw", "gpo", "gco", "w_out", "g1", "g2", "g3",
                 "g4", "w_up", "fcw", "w_down", "gf")


def _layer_kernel(xp_hbm, xs_hbm, ps_ref, cs_ref, fs_ref, *refs, n_prompt_tiles, final):
    nw = len(_WEIGHT_NAMES)
    w = dict(zip(_WEIGHT_NAMES, refs[:nw]))
    (yp_hbm, ys_hbm, npool_ref, nconv_ref, nffn_ref,
     vhist, chist, fhist, xbuf, obuf, hbuf, zbuf, in_sem, out_sem) = refs[nw:]
    t = pl.program_id(0)
    sample_tile = n_prompt_tiles
    is_sample = t == sample_tile

    def copies(tile, slot, *, sample, to_hbm):
        prompt_hbm, sample_hbm, buf, sem = (
            (yp_hbm, ys_hbm, obuf, out_sem) if to_hbm else (xp_hbm, xs_hbm, xbuf, in_sem))
        out = []
        for b in range(BATCH):
            hbm = sample_hbm.at[b] if sample else prompt_hbm.at[b, pl.ds(tile * TILE_POS, TILE_POS), :]
            vmem = buf.at[slot, :, b, :]
            src, dst = (vmem, hbm) if to_hbm else (hbm, vmem)
            out.append(pltpu.make_async_copy(src, dst, sem.at[slot]))
        return out

    def tile_io(tile, to_hbm, action):
        n_slots = OUT_SLOTS if to_hbm else X_SLOTS
        slot = lax.rem(tile + X_SLOTS * OUT_SLOTS, n_slots)

        @pl.when((tile >= 0) & (tile < sample_tile))
        def _():
            for cp in copies(tile, slot, sample=False, to_hbm=to_hbm):
                getattr(cp, action)()

        @pl.when(tile == sample_tile)
        def _():
            for cp in copies(0, slot, sample=True, to_hbm=to_hbm):
                getattr(cp, action)()

    load_start = functools.partial(tile_io, to_hbm=False, action="start")
    load_wait = functools.partial(tile_io, to_hbm=False, action="wait")
    store_start = functools.partial(tile_io, to_hbm=True, action="start")
    store_wait = functools.partial(tile_io, to_hbm=True, action="wait")

    def x_slot(tile):
        return xbuf.at[lax.rem(tile, X_SLOTS)]

    def project(tile, part):
        cols = slice(part * D_POOL, (part + 1) * D_POOL)
        zbuf[lax.rem(tile, Z_SLOTS), :, cols] = jnp.dot(
            hbuf[...], w["w_in"][:, cols], preferred_element_type=F32)

    @pl.when(t == 0)
    def _():
        load_start(t)
        load_start(t + 1)
        load_wait(t)
        hbuf[...] = _rms(_tile_rows(x_slot(t)), w["g1"][...]).astype(BF16)
        for part in range(D_IN // D_POOL):
            project(t, part)

    load_start(t + 2)
    load_wait(t + 1)
    store_wait(t - OUT_SLOTS)

    @pl.when((t == 0) | is_sample)
    def _():
        group = jnp.where(is_sample, SAMPLE, PROMPT)
        vhist[...] = ps_ref[group]
        chist[...] = cs_ref[group]
        fhist[...] = fs_ref[group]

    hbuf[...] = _rms(_tile_rows(x_slot(t + 1)), w["g1"][...]).astype(BF16)
    project(t + 1, 0)
    project(t + 1, 1)

    x_ref = x_slot(t)
    z = zbuf.at[lax.rem(t, Z_SLOTS)]
    v = z[:, :D_POOL]
    gb = z[:, D_POOL:D_POOL + D_CONV]
    cu = z[:, D_POOL + D_CONV:D_POOL + 2 * D_CONV] * z[:, D_POOL + 2 * D_CONV:]

    base = jnp.where(is_sample, PAST_LEN, t * TILE_POS)
    pos = base + lax.broadcasted_iota(jnp.int32, (TILE_ROWS, POOL_GROUP), 0) // BATCH
    pieces = []
    for gi, win in enumerate(POOL_WINDOWS):
        cols = slice(gi * POOL_GROUP, (gi + 1) * POOL_GROUP)
        cur = v[:, cols]
        s = jnp.concatenate([vhist[:, cols], cur], axis=0)
        span = 1
        while span < win:
            s = s[span * BATCH:] + s[:s.shape[0] - span * BATCH]
            span *= 2
        s = s[s.shape[0] - TILE_ROWS:]
        cnt = jnp.minimum(pos + 1, win).astype(F32)
        pooled = s / cnt - cur
        ya = jnp.dot(pooled.astype(BF16), w["pmix"][gi], preferred_element_type=F32)
        ya = ya * w["pscale"][:, cols]
        pieces.append(_rms(ya, w["gpo"][:, cols]).astype(BF16))
    pool_tail = v[TILE_ROWS - POOL_HIST * BATCH:]
    group = jnp.where(is_sample, SAMPLE, PROMPT)
    npool_ref[group] = pool_tail
    vhist[...] = pool_tail

    yb = gb * _causal_conv(chist[...], cu, w["convw"][...])
    conv_tail = cu[TILE_ROWS - CONV_HIST * BATCH:]
    nconv_ref[group] = conv_tail
    chist[...] = conv_tail
    sq = (yb * yb).astype(BF16)
    ri = lax.broadcasted_iota(jnp.int32, (SEG_BLOCK, SEG_BLOCK), 0) // CONV_HEAD
    ci = lax.broadcasted_iota(jnp.int32, (SEG_BLOCK, SEG_BLOCK), 1) // CONV_HEAD
    seg = jnp.where(ri == ci, 1.0, 0.0).astype(BF16)
    for j in range(D_CONV // SEG_BLOCK):
        cols = slice(j * SEG_BLOCK, (j + 1) * SEG_BLOCK)
        ss = jnp.dot(sq[:, cols], seg, preferred_element_type=F32)
        ybn = yb[:, cols] * lax.rsqrt(ss * (1.0 / CONV_HEAD) + EPS) * w["gco"][:, cols]
        pieces.append(ybn.astype(BF16))

    cat = jnp.concatenate(pieces, axis=-1)
    mix = jnp.dot(cat, w["w_out"][...], preferred_element_type=F32)

    x1 = _tile_rows(x_ref) + _rms(mix, w["g2"][...])
    h2 = _rms(x1, w["g3"][...]).astype(BF16)
    project(t + 1, 2)

    acts = []
    for c in range(N_FF_CHUNKS):
        halves = []
        for lo in (c * FF_CHUNK, D_FF + c * FF_CHUNK):
            cols = slice(lo, lo + FF_CHUNK)
            up = jnp.dot(h2, w["w_up"][:, cols], preferred_element_type=F32)
            halves.append(_causal_conv(fhist[:, cols], up, w["fcw"][:, cols]))
            tail = up[TILE_ROWS - CONV_HIST * BATCH:]
            nffn_ref[group, :, cols] = tail
            fhist[:, cols] = tail
        gate, val = halves
        acts.append((gate / (1.0 + jnp.exp(-gate)) * val).astype(BF16))
    act = jnp.concatenate(acts, axis=-1)
    f = jnp.dot(act, w["w_down"][...], preferred_element_type=F32)
    out = x1 + _rms(f, w["g4"][...])
    if final:
        out = _rms(out, w["gf"][...])
    project(t + 1, 3)
    obuf[lax.rem(t, OUT_SLOTS)] = out.reshape(TILE_POS, BATCH, D_MODEL)

    store_start(t)

    @pl.when(is_sample)
    def _():
        store_wait(t - 1)
        store_wait(t)


def _layer_call(xp, xs, states, weights, layer, *, final):
    n_prompt_tiles = xp.shape[1] // TILE_POS
    assert n_prompt_tiles >= X_SLOTS
    kern = functools.partial(_layer_kernel, n_prompt_tiles=n_prompt_tiles, final=final)

    def state_spec(a):
        return pl.BlockSpec((None,) + a.shape[1:], lambda t: (layer, 0, 0, 0))

    def out_state_spec(a):
        return pl.BlockSpec(a.shape[1:], lambda t: (0, 0, 0))

    def weight_spec(a):
        if a.ndim == 2:
            return pl.BlockSpec(a.shape, lambda t: (0, 0), pipeline_mode=pl.Buffered(1))
        zeros = (0,) * (a.ndim - 1)
        return pl.BlockSpec((None,) + a.shape[1:], lambda t: (layer,) + zeros,
                            pipeline_mode=pl.Buffered(1))

    any_spec = pl.BlockSpec(memory_space=pl.ANY)
    return pl.pallas_call(
        kern,
        out_shape=(jax.ShapeDtypeStruct(xp.shape, F32), jax.ShapeDtypeStruct(xs.shape, F32))
        + tuple(jax.ShapeDtypeStruct(s.shape[1:], F32) for s in states),
        grid=(n_prompt_tiles + 1,),
        in_specs=[any_spec, any_spec] + [state_spec(s) for s in states]
        + [weight_spec(a) for a in weights],
        out_specs=(any_spec, any_spec) + tuple(out_state_spec(s) for s in states),
        scratch_shapes=[pltpu.VMEM(s.shape[2:], F32) for s in states] + [
            pltpu.VMEM((X_SLOTS, TILE_POS, BATCH, D_MODEL), F32),
            pltpu.VMEM((OUT_SLOTS, TILE_POS, BATCH, D_MODEL), F32),
            pltpu.VMEM((TILE_ROWS, D_MODEL), BF16),
            pltpu.VMEM((Z_SLOTS, TILE_ROWS, D_IN), F32),
            pltpu.SemaphoreType.DMA((X_SLOTS,)),
            pltpu.SemaphoreType.DMA((OUT_SLOTS,)),
        ],
        compiler_params=pltpu.CompilerParams(
            dimension_semantics=("arbitrary",),
            vmem_limit_bytes=VMEM_LIMIT_BYTES),
        name="encoder_layer",
    )(xp, xs, *states, *weights)


def _batch_major(a, hist):
    a = a.reshape(a.shape[:-2] + (hist, BATCH, a.shape[-1]))
    return jnp.swapaxes(a, -3, -2)


def _state_in(s, hist):
    s = jnp.swapaxes(s, 1, 2)
    s = jnp.pad(s, ((0, 0), (hist - s.shape[1], 0), (0, 0), (0, 0)))
    s = s.reshape(s.shape[0], hist * BATCH, s.shape[-1])
    return jnp.stack([jnp.zeros_like(s), s], axis=1)


def kernel(x_prompt, x_sample, state_pool, state_conv, state_ffn_conv, w_in, pool_mix,
           pool_scale, conv_w, g_pool_out, g_conv_out, w_out, g_pre_mix, g_post_mix,
           g_pre_ffn, g_post_ffn, w_up, ffn_conv_w, w_down, g_final):
    depth = w_in.shape[0]
    bp, sp, _ = x_prompt.shape
    bs, ss, _ = x_sample.shape
    assert bp == BATCH and bs == BATCH and sp % TILE_POS == 0 and ss == TILE_POS

    row = lambda a: a[:, None, :]
    weights = dict(
        w_in=w_in.astype(BF16), pmix=pool_mix.astype(BF16), pscale=row(pool_scale), convw=conv_w,
        gpo=row(g_pool_out), gco=row(g_conv_out), w_out=w_out.astype(BF16), g1=row(g_pre_mix),
        g2=row(g_post_mix), g3=row(g_pre_ffn), g4=row(g_post_ffn), w_up=w_up.astype(BF16),
        fcw=ffn_conv_w, w_down=w_down.astype(BF16), gf=g_final[None, :])
    weights = tuple(weights[n] for n in _WEIGHT_NAMES)
    states = (_state_in(state_pool, POOL_HIST), _state_in(state_conv, CONV_HIST),
              _state_in(state_ffn_conv, CONV_HIST))

    xp, xs = x_prompt, x_sample
    pools, convs, ffns = [], [], []
    for l in range(depth):
        xp, xs, a, c, f = _layer_call(xp, xs, states, weights, l, final=(l == depth - 1))
        pools.append(a)
        convs.append(c)
        ffns.append(f)

    pools = _batch_major(jnp.stack(pools), POOL_HIST)[..., POOL_HIST - POOL_STATE:, :]
    convs = _batch_major(jnp.stack(convs), CONV_HIST)
    ffns = _batch_major(jnp.stack(ffns), CONV_HIST)
    return (xp, xs, pools[:, PROMPT], convs[:, PROMPT], ffns[:, PROMPT],
            pools[:, SAMPLE], convs[:, SAMPLE], ffns[:, SAMPLE])
```

```python
import functools

import jax
import jax.numpy as jnp
from jax import lax
from jax.experimental import pallas as pl
from jax.experimental.pallas import tpu as pltpu

D_MODEL = 1024
D_POOL = 512
D_CONV = 512
POOL_WINDOWS = (2, 4, 8, 16)
POOL_GROUP = 128
POOL_STATE = 15
CONV_HEAD = 64
CONV_WIDTH = 3
D_FF = 2816
D_IN = 2048
PAST_LEN = 1024
EPS = 1e-6

BATCH = 8
TILE_POS = 64
TILE_ROWS = TILE_POS * BATCH
POOL_HIST = POOL_STATE + 1
CONV_HIST = CONV_WIDTH - 1
FF_CHUNK = 256
N_FF_CHUNKS = D_FF // FF_CHUNK
SEG_BLOCK = 256
X_SLOTS = 3
OUT_SLOTS = 2
Z_SLOTS = 2
PROMPT, SAMPLE = 0, 1
MATMUL_WEIGHTS = {"w_in": 128, "w_out": 256, "w_up": 64, "w_down": 256}
VMEM_LIMIT_BYTES = 58 * 1024 * 1024

BF16 = jnp.bfloat16
F32 = jnp.float32


def _rms(x, g):
    ms = jnp.mean(x * x, axis=-1, keepdims=True)
    return x * lax.rsqrt(ms + EPS) * g


def _shift(hist, cur, k):
    if k == 0:
        return cur
    n = cur.shape[0]
    return jnp.concatenate([hist[hist.shape[0] - k * BATCH:], cur[:n - k * BATCH]], axis=0)


def _causal_conv(hist, cur, w):
    out = w[CONV_WIDTH - 1:CONV_WIDTH] * cur
    for k in range(1, CONV_WIDTH):
        out = out + w[CONV_WIDTH - 1 - k:CONV_WIDTH - k] * _shift(hist, cur, k)
    return out


def _tile_rows(ref):
    return ref[...].reshape(TILE_ROWS, D_MODEL)


_WEIGHT_NAMES = ("w_in", "pmix", "pscale", "convw", "gpo", "gco", "w_out", "g1", "g2", "g3",
                 "g4", "w_up", "fcw", "w_down", "gf")


def _cast_weight(src_hbm, dst, chunk_rows):
    rows, cols = dst.shape
    n_chunks = rows // chunk_rows
    assert n_chunks * chunk_rows == rows

    def body(stage, sem):
        def chunk_copy(i):
            return pltpu.make_async_copy(src_hbm.at[pl.ds(i * chunk_rows, chunk_rows), :],
                                         stage.at[i % 2], sem.at[i % 2])

        chunk_copy(0).start()

        def step(i, carry):
            @pl.when(i + 1 < n_chunks)
            def _():
                chunk_copy(i + 1).start()

            chunk_copy(i).wait()
            start = pl.multiple_of(i * chunk_rows, chunk_rows)
            dst[pl.ds(start, chunk_rows), :] = stage[i % 2].astype(BF16)
            return carry

        lax.fori_loop(0, n_chunks, step, 0)

    pl.run_scoped(body, pltpu.VMEM((2, chunk_rows, cols), F32), pltpu.SemaphoreType.DMA((2,)))


def _layer_kernel(xp_hbm, xs_hbm, ps_ref, cs_ref, fs_ref, *refs, n_prompt_tiles, layer, final):
    nw = len(_WEIGHT_NAMES)
    w = dict(zip(_WEIGHT_NAMES, refs[:nw]))
    (yp_hbm, ys_hbm, npool_ref, nconv_ref, nffn_ref,
     vhist, chist, fhist, xbuf, obuf, hbuf, zbuf, in_sem, out_sem, *wbufs) = refs[nw:]
    w_hbm = {name: w[name] for name in MATMUL_WEIGHTS}
    w.update(zip(MATMUL_WEIGHTS, wbufs))
    t = pl.program_id(0)
    sample_tile = n_prompt_tiles
    is_sample = t == sample_tile

    def copies(tile, slot, *, sample, to_hbm):
        prompt_hbm, sample_hbm, buf, sem = (
            (yp_hbm, ys_hbm, obuf, out_sem) if to_hbm else (xp_hbm, xs_hbm, xbuf, in_sem))
        out = []
        for b in range(BATCH):
            hbm = sample_hbm.at[b] if sample else prompt_hbm.at[b, pl.ds(tile * TILE_POS, TILE_POS), :]
            vmem = buf.at[slot, :, b, :]
            src, dst = (vmem, hbm) if to_hbm else (hbm, vmem)
            out.append(pltpu.make_async_copy(src, dst, sem.at[slot]))
        return out

    def tile_io(tile, to_hbm, action):
        n_slots = OUT_SLOTS if to_hbm else X_SLOTS
        slot = lax.rem(tile + X_SLOTS * OUT_SLOTS, n_slots)

        @pl.when((tile >= 0) & (tile < sample_tile))
        def _():
            for cp in copies(tile, slot, sample=False, to_hbm=to_hbm):
                getattr(cp, action)()

        @pl.when(tile == sample_tile)
        def _():
            for cp in copies(0, slot, sample=True, to_hbm=to_hbm):
                getattr(cp, action)()

    load_start = functools.partial(tile_io, to_hbm=False, action="start")
    load_wait = functools.partial(tile_io, to_hbm=False, action="wait")
    store_start = functools.partial(tile_io, to_hbm=True, action="start")
    store_wait = functools.partial(tile_io, to_hbm=True, action="wait")

    def x_slot(tile):
        return xbuf.at[lax.rem(tile, X_SLOTS)]

    def project(tile, part):
        cols = slice(part * D_POOL, (part + 1) * D_POOL)
        zbuf[lax.rem(tile, Z_SLOTS), :, cols] = jnp.dot(
            hbuf[...], w["w_in"][:, cols], preferred_element_type=F32)

    @pl.when(t == 0)
    def _():
        load_start(t)
        load_start(t + 1)
        for name, chunk_rows in MATMUL_WEIGHTS.items():
            _cast_weight(w_hbm[name].at[layer], w[name], chunk_rows)
        load_wait(t)
        hbuf[...] = _rms(_tile_rows(x_slot(t)), w["g1"][...]).astype(BF16)
        for part in range(D_IN // D_POOL):
            project(t, part)

    load_start(t + 2)
    load_wait(t + 1)
    store_wait(t - OUT_SLOTS)

    @pl.when((t == 0) | is_sample)
    def _():
        group = jnp.where(is_sample, SAMPLE, PROMPT)
        vhist[...] = ps_ref[group]
        chist[...] = cs_ref[group]
        fhist[...] = fs_ref[group]

    hbuf[...] = _rms(_tile_rows(x_slot(t + 1)), w["g1"][...]).astype(BF16)
    project(t + 1, 0)
    project(t + 1, 1)

    x_ref = x_slot(t)
    z = zbuf.at[lax.rem(t, Z_SLOTS)]
    v = z[:, :D_POOL]
    gb = z[:, D_POOL:D_POOL + D_CONV]
    cu = z[:, D_POOL + D_CONV:D_POOL + 2 * D_CONV] * z[:, D_POOL + 2 * D_CONV:]

    base = jnp.where(is_sample, PAST_LEN, t * TILE_POS)
    pos = base + lax.broadcasted_iota(jnp.int32, (TILE_ROWS, POOL_GROUP), 0) // BATCH
    pieces = []
    for gi, win in enumerate(POOL_WINDOWS):
        cols = slice(gi * POOL_GROUP, (gi + 1) * POOL_GROUP)
        cur = v[:, cols]
        s = jnp.concatenate([vhist[:, cols], cur], axis=0)
        span = 1
        while span < win:
            s = s[span * BATCH:] + s[:s.shape[0] - span * BATCH]
            span *= 2
        s = s[s.shape[0] - TILE_ROWS:]
        cnt = jnp.minimum(pos + 1, win).astype(F32)
        pooled = s / cnt - cur
        ya = jnp.dot(pooled.astype(BF16), w["pmix"][gi], preferred_element_type=F32)
        ya = ya * w["pscale"][:, cols]
        pieces.append(_rms(ya, w["gpo"][:, cols]).astype(BF16))
    pool_tail = v[TILE_ROWS - POOL_HIST * BATCH:]
    group = jnp.where(is_sample, SAMPLE, PROMPT)
    npool_ref[group] = pool_tail
    vhist[...] = pool_tail

    yb = gb * _causal_conv(chist[...], cu, w["convw"][...])
    conv_tail = cu[TILE_ROWS - CONV_HIST * BATCH:]
    nconv_ref[group] = conv_tail
    chist[...] = conv_tail
    sq = (yb * yb).astype(BF16)
    ri = lax.broadcasted_iota(jnp.int32, (SEG_BLOCK, SEG_BLOCK), 0) // CONV_HEAD
    ci = lax.broadcasted_iota(jnp.int32, (SEG_BLOCK, SEG_BLOCK), 1) // CONV_HEAD
    seg = jnp.where(ri == ci, 1.0, 0.0).astype(BF16)
    for j in range(D_CONV // SEG_BLOCK):
        cols = slice(j * SEG_BLOCK, (j + 1) * SEG_BLOCK)
        ss = jnp.dot(sq[:, cols], seg, preferred_element_type=F32)
        ybn = yb[:, cols] * lax.rsqrt(ss * (1.0 / CONV_HEAD) + EPS) * w["gco"][:, cols]
        pieces.append(ybn.astype(BF16))

    cat = jnp.concatenate(pieces, axis=-1)
    mix = jnp.dot(cat, w["w_out"][...], preferred_element_type=F32)

    x1 = _tile_rows(x_ref) + _rms(mix, w["g2"][...])
    h2 = _rms(x1, w["g3"][...]).astype(BF16)
    project(t + 1, 2)

    acts = []
    for c in range(N_FF_CHUNKS):
        halves = []
        for lo in (c * FF_CHUNK, D_FF + c * FF_CHUNK):
            cols = slice(lo, lo + FF_CHUNK)
            up = jnp.dot(h2, w["w_up"][:, cols], preferred_element_type=F32)
            halves.append(_causal_conv(fhist[:, cols], up, w["fcw"][:, cols]))
            tail = up[TILE_ROWS - CONV_HIST * BATCH:]
            nffn_ref[group, :, cols] = tail
            fhist[:, cols] = tail
        gate, val = halves
        acts.append((gate / (1.0 + jnp.exp(-gate)) * val).astype(BF16))
    act = jnp.concatenate(acts, axis=-1)
    f = jnp.dot(act, w["w_down"][...], preferred_element_type=F32)
    out = x1 + _rms(f, w["g4"][...])
    if final:
        out = _rms(out, w["gf"][...])
    project(t + 1, 3)
    obuf[lax.rem(t, OUT_SLOTS)] = out.reshape(TILE_POS, BATCH, D_MODEL)

    store_start(t)

    @pl.when(is_sample)
    def _():
        store_wait(t - 1)
        store_wait(t)


def _layer_call(xp, xs, states, weights, layer, *, final):
    n_prompt_tiles = xp.shape[1] // TILE_POS
    assert n_prompt_tiles >= X_SLOTS
    kern = functools.partial(_layer_kernel, n_prompt_tiles=n_prompt_tiles, layer=layer,
                             final=final)

    def state_spec(a):
        return pl.BlockSpec((None,) + a.shape[1:], lambda t: (layer, 0, 0, 0))

    def out_state_spec(a):
        return pl.BlockSpec(a.shape[1:], lambda t: (0, 0, 0))

    def weight_spec(a):
        if a.ndim == 2:
            return pl.BlockSpec(a.shape, lambda t: (0, 0), pipeline_mode=pl.Buffered(1))
        zeros = (0,) * (a.ndim - 1)
        return pl.BlockSpec((None,) + a.shape[1:], lambda t: (layer,) + zeros,
                            pipeline_mode=pl.Buffered(1))

    any_spec = pl.BlockSpec(memory_space=pl.ANY)
    named = dict(zip(_WEIGHT_NAMES, weights))
    weight_specs = [any_spec if n in MATMUL_WEIGHTS else weight_spec(named[n])
                    for n in _WEIGHT_NAMES]
    return pl.pallas_call(
        kern,
        out_shape=(jax.ShapeDtypeStruct(xp.shape, F32), jax.ShapeDtypeStruct(xs.shape, F32))
        + tuple(jax.ShapeDtypeStruct(s.shape[1:], F32) for s in states),
        grid=(n_prompt_tiles + 1,),
        in_specs=[any_spec, any_spec] + [state_spec(s) for s in states]
        + weight_specs,
        out_specs=(any_spec, any_spec) + tuple(out_state_spec(s) for s in states),
        scratch_shapes=[pltpu.VMEM(s.shape[2:], F32) for s in states] + [
            pltpu.VMEM((X_SLOTS, TILE_POS, BATCH, D_MODEL), F32),
            pltpu.VMEM((OUT_SLOTS, TILE_POS, BATCH, D_MODEL), F32),
            pltpu.VMEM((TILE_ROWS, D_MODEL), BF16),
            pltpu.VMEM((Z_SLOTS, TILE_ROWS, D_IN), F32),
            pltpu.SemaphoreType.DMA((X_SLOTS,)),
            pltpu.SemaphoreType.DMA((OUT_SLOTS,)),
        ] + [pltpu.VMEM(named[n].shape[1:], BF16) for n in MATMUL_WEIGHTS],
        compiler_params=pltpu.CompilerParams(
            dimension_semantics=("arbitrary",),
            vmem_limit_bytes=VMEM_LIMIT_BYTES),
        name="encoder_layer",
    )(xp, xs, *states, *weights)


def _batch_major(a, hist):
    a = a.reshape(a.shape[:-2] + (hist, BATCH, a.shape[-1]))
    return jnp.swapaxes(a, -3, -2)


def _state_in(s, hist):
    s = jnp.swapaxes(s, 1, 2)
    s = jnp.pad(s, ((0, 0), (hist - s.shape[1], 0), (0, 0), (0, 0)))
    s = s.reshape(s.shape[0], hist * BATCH, s.shape[-1])
    return jnp.stack([jnp.zeros_like(s), s], axis=1)


def kernel(x_prompt, x_sample, state_pool, state_conv, state_ffn_conv, w_in, pool_mix,
           pool_scale, conv_w, g_pool_out, g_conv_out, w_out, g_pre_mix, g_post_mix,
           g_pre_ffn, g_post_ffn, w_up, ffn_conv_w, w_down, g_final):
    depth = w_in.shape[0]
    bp, sp, _ = x_prompt.shape
    bs, ss, _ = x_sample.shape
    assert bp == BATCH and bs == BATCH and sp % TILE_POS == 0 and ss == TILE_POS

    row = lambda a: a[:, None, :]
    weights = dict(
        w_in=w_in, pmix=pool_mix.astype(BF16), pscale=row(pool_scale), convw=conv_w,
        gpo=row(g_pool_out), gco=row(g_conv_out), w_out=w_out, g1=row(g_pre_mix),
        g2=row(g_post_mix), g3=row(g_pre_ffn), g4=row(g_post_ffn), w_up=w_up,
        fcw=ffn_conv_w, w_down=w_down, gf=g_final[None, :])
    weights = tuple(weights[n] for n in _WEIGHT_NAMES)
    states = (_state_in(state_pool, POOL_HIST), _state_in(state_conv, CONV_HIST),
              _state_in(state_ffn_conv, CONV_HIST))

    xp, xs = x_prompt, x_sample
    pools, convs, ffns = [], [], []
    for l in range(depth):
        xp, xs, a, c, f = _layer_call(xp, xs, states, weights, l, final=(l == depth - 1))
        pools.append(a)
        convs.append(c)
        ffns.append(f)

    pools = _batch_major(jnp.stack(pools), POOL_HIST)[..., POOL_HIST - POOL_STATE:, :]
    convs = _batch_major(jnp.stack(convs), CONV_HIST)
    ffns = _batch_major(jnp.stack(ffns), CONV_HIST)
    return (xp, xs, pools[:, PROMPT], convs[:, PROMPT], ffns[:, PROMPT],
            pools[:, SAMPLE], convs[:, SAMPLE], ffns[:, SAMPLE])
```

```python
import functools

import jax
import jax.numpy as jnp
from jax import lax
from jax.experimental import pallas as pl
from jax.experimental.pallas import tpu as pltpu

D_MODEL = 1024
D_POOL = 512
D_CONV = 512
POOL_WINDOWS = (2, 4, 8, 16)
POOL_GROUP = 128
POOL_STATE = 15
CONV_HEAD = 64
CONV_WIDTH = 3
D_FF = 2816
D_IN = 2048
PAST_LEN = 1024
EPS = 1e-6

BATCH = 8
TILE_POS = 64
TILE_ROWS = TILE_POS * BATCH
POOL_HIST = POOL_STATE + 1
CONV_HIST = CONV_WIDTH - 1
FF_CHUNK = 256
N_FF_CHUNKS = D_FF // FF_CHUNK
SEG_BLOCK = 256
X_SLOTS = 3
OUT_SLOTS = 2
Z_SLOTS = 2
PROMPT, SAMPLE = 0, 1
MATMUL_WEIGHTS = {"w_in": 128, "w_out": 256, "w_up": 32, "w_down": 256}
STAGE_SLOTS = 4
VMEM_LIMIT_BYTES = 58 * 1024 * 1024

BF16 = jnp.bfloat16
F32 = jnp.float32


def _rms(x, g):
    ms = jnp.mean(x * x, axis=-1, keepdims=True)
    return x * lax.rsqrt(ms + EPS) * g


def _shift(hist, cur, k):
    if k == 0:
        return cur
    n = cur.shape[0]
    return jnp.concatenate([hist[hist.shape[0] - k * BATCH:], cur[:n - k * BATCH]], axis=0)


def _causal_conv(hist, cur, w):
    out = w[CONV_WIDTH - 1:CONV_WIDTH] * cur
    for k in range(1, CONV_WIDTH):
        out = out + w[CONV_WIDTH - 1 - k:CONV_WIDTH - k] * _shift(hist, cur, k)
    return out


def _tile_rows(ref):
    return ref[...].reshape(TILE_ROWS, D_MODEL)


_WEIGHT_NAMES = ("w_in", "pmix", "pscale", "convw", "gpo", "gco", "w_out", "g1", "g2", "g3",
                 "g4", "w_up", "fcw", "w_down", "gf")


def _cast_weight(src_hbm, dst, chunk_rows):
    rows, cols = dst.shape
    n_chunks = rows // chunk_rows
    assert n_chunks * chunk_rows == rows

    def body(stage, sem):
        def chunk_copy(i):
            return pltpu.make_async_copy(src_hbm.at[pl.ds(i * chunk_rows, chunk_rows), :],
                                         stage.at[i % STAGE_SLOTS], sem.at[i % STAGE_SLOTS])

        for i in range(min(STAGE_SLOTS - 1, n_chunks)):
            chunk_copy(i).start()

        def step(i, carry):
            @pl.when(i + STAGE_SLOTS - 1 < n_chunks)
            def _():
                chunk_copy(i + STAGE_SLOTS - 1).start()

            chunk_copy(i).wait()
            start = pl.multiple_of(i * chunk_rows, chunk_rows)
            dst[pl.ds(start, chunk_rows), :] = stage[i % STAGE_SLOTS].astype(BF16)
            return carry

        lax.fori_loop(0, n_chunks, step, 0)

    pl.run_scoped(body, pltpu.VMEM((STAGE_SLOTS, chunk_rows, cols), F32),
                  pltpu.SemaphoreType.DMA((STAGE_SLOTS,)))


def _layer_kernel(xp_hbm, xs_hbm, ps_ref, cs_ref, fs_ref, *refs, n_prompt_tiles, layer, final):
    nw = len(_WEIGHT_NAMES)
    w = dict(zip(_WEIGHT_NAMES, refs[:nw]))
    (yp_hbm, ys_hbm, npool_ref, nconv_ref, nffn_ref,
     vhist, chist, fhist, xbuf, obuf, hbuf, zbuf, in_sem, out_sem, *wbufs) = refs[nw:]
    w_hbm = {name: w[name] for name in MATMUL_WEIGHTS}
    w.update(zip(MATMUL_WEIGHTS, wbufs))
    t = pl.program_id(0)
    sample_tile = n_prompt_tiles
    is_sample = t == sample_tile

    def copies(tile, slot, *, sample, to_hbm):
        prompt_hbm, sample_hbm, buf, sem = (
            (yp_hbm, ys_hbm, obuf, out_sem) if to_hbm else (xp_hbm, xs_hbm, xbuf, in_sem))
        out = []
        for b in range(BATCH):
            hbm = sample_hbm.at[b] if sample else prompt_hbm.at[b, pl.ds(tile * TILE_POS, TILE_POS), :]
            vmem = buf.at[slot, :, b, :]
            src, dst = (vmem, hbm) if to_hbm else (hbm, vmem)
            out.append(pltpu.make_async_copy(src, dst, sem.at[slot]))
        return out

    def tile_io(tile, to_hbm, action):
        n_slots = OUT_SLOTS if to_hbm else X_SLOTS
        slot = lax.rem(tile + X_SLOTS * OUT_SLOTS, n_slots)

        @pl.when((tile >= 0) & (tile < sample_tile))
        def _():
            for cp in copies(tile, slot, sample=False, to_hbm=to_hbm):
                getattr(cp, action)()

        @pl.when(tile == sample_tile)
        def _():
            for cp in copies(0, slot, sample=True, to_hbm=to_hbm):
                getattr(cp, action)()

    load_start = functools.partial(tile_io, to_hbm=False, action="start")
    load_wait = functools.partial(tile_io, to_hbm=False, action="wait")
    store_start = functools.partial(tile_io, to_hbm=True, action="start")
    store_wait = functools.partial(tile_io, to_hbm=True, action="wait")

    def x_slot(tile):
        return xbuf.at[lax.rem(tile, X_SLOTS)]

    def project(tile, part):
        cols = slice(part * D_POOL, (part + 1) * D_POOL)
        zbuf[lax.rem(tile, Z_SLOTS), :, cols] = jnp.dot(
            hbuf[...], w["w_in"][:, cols], preferred_element_type=F32)

    @pl.when(t == 0)
    def _():
        load_start(t)
        load_start(t + 1)
        for name, chunk_rows in MATMUL_WEIGHTS.items():
            _cast_weight(w_hbm[name].at[layer], w[name], chunk_rows)
        load_wait(t)
        hbuf[...] = _rms(_tile_rows(x_slot(t)), w["g1"][...]).astype(BF16)
        for part in range(D_IN // D_POOL):
            project(t, part)

    load_start(t + 2)
    load_wait(t + 1)
    store_wait(t - OUT_SLOTS)

    @pl.when((t == 0) | is_sample)
    def _():
        group = jnp.where(is_sample, SAMPLE, PROMPT)
        vhist[...] = ps_ref[group]
        chist[...] = cs_ref[group]
        fhist[...] = fs_ref[group]

    hbuf[...] = _rms(_tile_rows(x_slot(t + 1)), w["g1"][...]).astype(BF16)
    project(t + 1, 0)
    project(t + 1, 1)

    x_ref = x_slot(t)
    z = zbuf.at[lax.rem(t, Z_SLOTS)]
    v = z[:, :D_POOL]
    gb = z[:, D_POOL:D_POOL + D_CONV]
    cu = z[:, D_POOL + D_CONV:D_POOL + 2 * D_CONV] * z[:, D_POOL + 2 * D_CONV:]

    base = jnp.where(is_sample, PAST_LEN, t * TILE_POS)
    pos = base + lax.broadcasted_iota(jnp.int32, (TILE_ROWS, POOL_GROUP), 0) // BATCH
    pieces = []
    for gi, win in enumerate(POOL_WINDOWS):
        cols = slice(gi * POOL_GROUP, (gi + 1) * POOL_GROUP)
        cur = v[:, cols]
        s = jnp.concatenate([vhist[:, cols], cur], axis=0)
        span = 1
        while span < win:
            s = s[span * BATCH:] + s[:s.shape[0] - span * BATCH]
            span *= 2
        s = s[s.shape[0] - TILE_ROWS:]
        cnt = jnp.minimum(pos + 1, win).astype(F32)
        pooled = s / cnt - cur
        ya = jnp.dot(pooled.astype(BF16), w["pmix"][gi], preferred_element_type=F32)
        ya = ya * w["pscale"][:, cols]
        pieces.append(_rms(ya, w["gpo"][:, cols]).astype(BF16))
    pool_tail = v[TILE_ROWS - POOL_HIST * BATCH:]
    group = jnp.where(is_sample, SAMPLE, PROMPT)
    npool_ref[group] = pool_tail
    vhist[...] = pool_tail

    yb = gb * _causal_conv(chist[...], cu, w["convw"][...])
    conv_tail = cu[TILE_ROWS - CONV_HIST * BATCH:]
    nconv_ref[group] = conv_tail
    chist[...] = conv_tail
    sq = (yb * yb).astype(BF16)
    ri = lax.broadcasted_iota(jnp.int32, (SEG_BLOCK, SEG_BLOCK), 0) // CONV_HEAD
    ci = lax.broadcasted_iota(jnp.int32, (SEG_BLOCK, SEG_BLOCK), 1) // CONV_HEAD
    seg = jnp.where(ri == ci, 1.0, 0.0).astype(BF16)
    for j in range(D_CONV // SEG_BLOCK):
        cols = slice(j * SEG_BLOCK, (j + 1) * SEG_BLOCK)
        ss = jnp.dot(sq[:, cols], seg, preferred_element_type=F32)
        ybn = yb[:, cols] * lax.rsqrt(ss * (1.0 / CONV_HEAD) + EPS) * w["gco"][:, cols]
        pieces.append(ybn.astype(BF16))

    cat = jnp.concatenate(pieces, axis=-1)
    mix = jnp.dot(cat, w["w_out"][...], preferred_element_type=F32)

    x1 = _tile_rows(x_ref) + _rms(mix, w["g2"][...])
    h2 = _rms(x1, w["g3"][...]).astype(BF16)
    project(t + 1, 2)

    acts = []
    for c in range(N_FF_CHUNKS):
        halves = []
        for lo in (c * FF_CHUNK, D_FF + c * FF_CHUNK):
            cols = slice(lo, lo + FF_CHUNK)
            up = jnp.dot(h2, w["w_up"][:, cols], preferred_element_type=F32)
            halves.append(_causal_conv(fhist[:, cols], up, w["fcw"][:, cols]))
            tail = up[TILE_ROWS - CONV_HIST * BATCH:]
            nffn_ref[group, :, cols] = tail
            fhist[:, cols] = tail
        gate, val = halves
        acts.append((gate / (1.0 + jnp.exp(-gate)) * val).astype(BF16))
    act = jnp.concatenate(acts, axis=-1)
    f = jnp.dot(act, w["w_down"][...], preferred_element_type=F32)
    out = x1 + _rms(f, w["g4"][...])
    if final:
        out = _rms(out, w["gf"][...])
    project(t + 1, 3)
    obuf[lax.rem(t, OUT_SLOTS)] = out.reshape(TILE_POS, BATCH, D_MODEL)

    store_start(t)

    @pl.when(is_sample)
    def _():
        store_wait(t - 1)
        store_wait(t)


def _layer_call(xp, xs, states, weights, layer, *, final):
    n_prompt_tiles = xp.shape[1] // TILE_POS
    assert n_prompt_tiles >= X_SLOTS
    kern = functools.partial(_layer_kernel, n_prompt_tiles=n_prompt_tiles, layer=layer,
                             final=final)

    def state_spec(a):
        return pl.BlockSpec((None,) + a.shape[1:], lambda t: (layer, 0, 0, 0))

    def out_state_spec(a):
        return pl.BlockSpec(a.shape[1:], lambda t: (0, 0, 0))

    def weight_spec(a):
        if a.ndim == 2:
            return pl.BlockSpec(a.shape, lambda t: (0, 0), pipeline_mode=pl.Buffered(1))
        zeros = (0,) * (a.ndim - 1)
        return pl.BlockSpec((None,) + a.shape[1:], lambda t: (layer,) + zeros,
                            pipeline_mode=pl.Buffered(1))

    any_spec = pl.BlockSpec(memory_space=pl.ANY)
    named = dict(zip(_WEIGHT_NAMES, weights))
    weight_specs = [any_spec if n in MATMUL_WEIGHTS else weight_spec(named[n])
                    for n in _WEIGHT_NAMES]
    return pl.pallas_call(
        kern,
        out_shape=(jax.ShapeDtypeStruct(xp.shape, F32), jax.ShapeDtypeStruct(xs.shape, F32))
        + tuple(jax.ShapeDtypeStruct(s.shape[1:], F32) for s in states),
        grid=(n_prompt_tiles + 1,),
        in_specs=[any_spec, any_spec] + [state_spec(s) for s in states]
        + weight_specs,
        out_specs=(any_spec, any_spec) + tuple(out_state_spec(s) for s in states),
        scratch_shapes=[pltpu.VMEM(s.shape[2:], F32) for s in states] + [
            pltpu.VMEM((X_SLOTS, TILE_POS, BATCH, D_MODEL), F32),
            pltpu.VMEM((OUT_SLOTS, TILE_POS, BATCH, D_MODEL), F32),
            pltpu.VMEM((TILE_ROWS, D_MODEL), BF16),
            pltpu.VMEM((Z_SLOTS, TILE_ROWS, D_IN), F32),
            pltpu.SemaphoreType.DMA((X_SLOTS,)),
            pltpu.SemaphoreType.DMA((OUT_SLOTS,)),
        ] + [pltpu.VMEM(named[n].shape[1:], BF16) for n in MATMUL_WEIGHTS],
        compiler_params=pltpu.CompilerParams(
            dimension_semantics=("arbitrary",),
            vmem_limit_bytes=VMEM_LIMIT_BYTES),
        name="encoder_layer",
    )(xp, xs, *states, *weights)


def _batch_major(a, hist):
    a = a.reshape(a.shape[:-2] + (hist, BATCH, a.shape[-1]))
    return jnp.swapaxes(a, -3, -2)


def _state_in(s, hist):
    s = jnp.swapaxes(s, 1, 2)
    s = jnp.pad(s, ((0, 0), (hist - s.shape[1], 0), (0, 0), (0, 0)))
    s = s.reshape(s.shape[0], hist * BATCH, s.shape[-1])
    return jnp.stack([jnp.zeros_like(s), s], axis=1)


def kernel(x_prompt, x_sample, state_pool, state_conv, state_ffn_conv, w_in, pool_mix,
           pool_scale, conv_w, g_pool_out, g_conv_out, w_out, g_pre_mix, g_post_mix,
           g_pre_ffn, g_post_ffn, w_up, ffn_conv_w, w_down, g_final):
    depth = w_in.shape[0]
    bp, sp, _ = x_prompt.shape
    bs, ss, _ = x_sample.shape
    assert bp == BATCH and bs == BATCH and sp % TILE_POS == 0 and ss == TILE_POS

    row = lambda a: a[:, None, :]
    weights = dict(
        w_in=w_in, pmix=pool_mix.astype(BF16), pscale=row(pool_scale), convw=conv_w,
        gpo=row(g_pool_out), gco=row(g_conv_out), w_out=w_out, g1=row(g_pre_mix),
        g2=row(g_post_mix), g3=row(g_pre_ffn), g4=row(g_post_ffn), w_up=w_up,
        fcw=ffn_conv_w, w_down=w_down, gf=g_final[None, :])
    weights = tuple(weights[n] for n in _WEIGHT_NAMES)
    states = (_state_in(state_pool, POOL_HIST), _state_in(state_conv, CONV_HIST),
              _state_in(state_ffn_conv, CONV_HIST))

    xp, xs = x_prompt, x_sample
    pools, convs, ffns = [], [], []
    for l in range(depth):
        xp, xs, a, c, f = _layer_call(xp, xs, states, weights, l, final=(l == depth - 1))
        pools.append(a)
        convs.append(c)
        ffns.append(f)

    pools = _batch_major(jnp.stack(pools), POOL_HIST)[..., POOL_HIST - POOL_STATE:, :]
    convs = _batch_major(jnp.stack(convs), CONV_HIST)
    ffns = _batch_major(jnp.stack(ffns), CONV_HIST)
    return (xp, xs, pools[:, PROMPT], convs[:, PROMPT], ffns[:, PROMPT],
            pools[:, SAMPLE], convs[:, SAMPLE], ffns[:, SAMPLE])
```

```python
import functools

import jax
import jax.numpy as jnp
from jax import lax
from jax.experimental import pallas as pl
from jax.experimental.pallas import tpu as pltpu

D_MODEL = 1024
D_POOL = 512
D_CONV = 512
POOL_WINDOWS = (2, 4, 8, 16)
POOL_GROUP = 128
POOL_STATE = 15
CONV_HEAD = 64
CONV_WIDTH = 3
D_FF = 2816
D_IN = 2048
PAST_LEN = 1024
EPS = 1e-6

BATCH = 8
TILE_POS = 64
TILE_ROWS = TILE_POS * BATCH
POOL_HIST = POOL_STATE + 1
CONV_HIST = CONV_WIDTH - 1
FF_CHUNK = 256
N_FF_CHUNKS = D_FF // FF_CHUNK
SEG_BLOCK = 256
X_SLOTS = 3
OUT_SLOTS = 2
Z_SLOTS = 2
PROMPT, SAMPLE = 0, 1
MATMUL_WEIGHTS = {"w_in": 128, "w_out": 256, "w_up": 32, "w_down": 256}
STAGE_SLOTS = 4
FOLDED_GAIN = {"w_in": "g1", "w_up": "g3"}
VMEM_LIMIT_BYTES = 58 * 1024 * 1024

BF16 = jnp.bfloat16
F32 = jnp.float32


def _unit_rms(x):
    ms = jnp.mean(x * x, axis=-1, keepdims=True)
    return x * lax.rsqrt(ms + EPS)


def _rms(x, g):
    return _unit_rms(x) * g


def _shift(hist, cur, k):
    if k == 0:
        return cur
    n = cur.shape[0]
    return jnp.concatenate([hist[hist.shape[0] - k * BATCH:], cur[:n - k * BATCH]], axis=0)


def _causal_conv(hist, cur, w):
    out = w[CONV_WIDTH - 1:CONV_WIDTH] * cur
    for k in range(1, CONV_WIDTH):
        out = out + w[CONV_WIDTH - 1 - k:CONV_WIDTH - k] * _shift(hist, cur, k)
    return out


def _tile_rows(ref):
    return ref[...].reshape(TILE_ROWS, D_MODEL)


_WEIGHT_NAMES = ("w_in", "pmix", "pscale", "convw", "gpo", "gco", "w_out", "g1", "g2", "g3",
                 "g4", "w_up", "fcw", "w_down", "gf")


def _cast_weight(src_hbm, dst, chunk_rows, row_gain=None):
    rows, cols = dst.shape
    n_chunks = rows // chunk_rows
    assert n_chunks * chunk_rows == rows

    def body(stage, sem):
        def chunk_copy(i):
            return pltpu.make_async_copy(src_hbm.at[pl.ds(i * chunk_rows, chunk_rows), :],
                                         stage.at[i % STAGE_SLOTS], sem.at[i % STAGE_SLOTS])

        for i in range(min(STAGE_SLOTS - 1, n_chunks)):
            chunk_copy(i).start()

        def step(i, carry):
            @pl.when(i + STAGE_SLOTS - 1 < n_chunks)
            def _():
                chunk_copy(i + STAGE_SLOTS - 1).start()

            chunk_copy(i).wait()
            start = pl.multiple_of(i * chunk_rows, chunk_rows)
            chunk = stage[i % STAGE_SLOTS]
            if row_gain is not None:
                chunk = chunk * row_gain[pl.ds(start, chunk_rows), :]
            dst[pl.ds(start, chunk_rows), :] = chunk.astype(BF16)
            return carry

        lax.fori_loop(0, n_chunks, step, 0)

    pl.run_scoped(body, pltpu.VMEM((STAGE_SLOTS, chunk_rows, cols), F32),
                  pltpu.SemaphoreType.DMA((STAGE_SLOTS,)))


def _layer_kernel(xp_hbm, xs_hbm, ps_ref, cs_ref, fs_ref, *refs, n_prompt_tiles, layer, final):
    nw = len(_WEIGHT_NAMES)
    w = dict(zip(_WEIGHT_NAMES, refs[:nw]))
    (yp_hbm, ys_hbm, npool_ref, nconv_ref, nffn_ref,
     vhist, chist, fhist, xbuf, obuf, hbuf, zbuf, in_sem, out_sem, *wbufs) = refs[nw:]
    w_hbm = {name: w[name] for name in MATMUL_WEIGHTS}
    w.update(zip(MATMUL_WEIGHTS, wbufs))
    t = pl.program_id(0)
    sample_tile = n_prompt_tiles
    is_sample = t == sample_tile

    def copies(tile, slot, *, sample, to_hbm):
        prompt_hbm, sample_hbm, buf, sem = (
            (yp_hbm, ys_hbm, obuf, out_sem) if to_hbm else (xp_hbm, xs_hbm, xbuf, in_sem))
        out = []
        for b in range(BATCH):
            hbm = sample_hbm.at[b] if sample else prompt_hbm.at[b, pl.ds(tile * TILE_POS, TILE_POS), :]
            vmem = buf.at[slot, :, b, :]
            src, dst = (vmem, hbm) if to_hbm else (hbm, vmem)
            out.append(pltpu.make_async_copy(src, dst, sem.at[slot]))
        return out

    def tile_io(tile, to_hbm, action):
        n_slots = OUT_SLOTS if to_hbm else X_SLOTS
        slot = lax.rem(tile + X_SLOTS * OUT_SLOTS, n_slots)

        @pl.when((tile >= 0) & (tile < sample_tile))
        def _():
            for cp in copies(tile, slot, sample=False, to_hbm=to_hbm):
                getattr(cp, action)()

        @pl.when(tile == sample_tile)
        def _():
            for cp in copies(0, slot, sample=True, to_hbm=to_hbm):
                getattr(cp, action)()

    load_start = functools.partial(tile_io, to_hbm=False, action="start")
    load_wait = functools.partial(tile_io, to_hbm=False, action="wait")
    store_start = functools.partial(tile_io, to_hbm=True, action="start")
    store_wait = functools.partial(tile_io, to_hbm=True, action="wait")

    def x_slot(tile):
        return xbuf.at[lax.rem(tile, X_SLOTS)]

    def project(tile, part):
        cols = slice(part * D_POOL, (part + 1) * D_POOL)
        zbuf[lax.rem(tile, Z_SLOTS), :, cols] = jnp.dot(
            hbuf[...], w["w_in"][:, cols], preferred_element_type=F32)

    @pl.when(t == 0)
    def _():
        load_start(t)
        load_start(t + 1)
        for name, chunk_rows in MATMUL_WEIGHTS.items():
            gain = w[FOLDED_GAIN[name]] if name in FOLDED_GAIN else None
            _cast_weight(w_hbm[name].at[layer], w[name], chunk_rows, gain)
        load_wait(t)
        hbuf[...] = _unit_rms(_tile_rows(x_slot(t))).astype(BF16)
        for part in range(D_IN // D_POOL):
            project(t, part)

    load_start(t + 2)
    load_wait(t + 1)
    store_wait(t - OUT_SLOTS)

    @pl.when(t == 0)
    def _():
        for hist in (vhist, chist, fhist):
            hist[...] = jnp.zeros(hist.shape, F32)

    @pl.when(is_sample)
    def _():
        vhist[...] = ps_ref[...]
        chist[...] = cs_ref[...]
        fhist[...] = fs_ref[...]

    hbuf[...] = _unit_rms(_tile_rows(x_slot(t + 1))).astype(BF16)

    x_ref = x_slot(t)
    z = zbuf.at[lax.rem(t, Z_SLOTS)]
    v = z[:, :D_POOL]
    gb = z[:, D_POOL:D_POOL + D_CONV]
    cu = z[:, D_POOL + D_CONV:D_POOL + 2 * D_CONV] * z[:, D_POOL + 2 * D_CONV:]

    base = jnp.where(is_sample, PAST_LEN, t * TILE_POS)
    pos = base + lax.broadcasted_iota(jnp.int32, (TILE_ROWS, POOL_GROUP), 0) // BATCH
    n_seen = (pos + 1).astype(F32)
    pieces = []
    for gi, win in enumerate(POOL_WINDOWS):
        cols = slice(gi * POOL_GROUP, (gi + 1) * POOL_GROUP)
        cur = v[:, cols]
        s = jnp.concatenate([vhist[:, cols], cur], axis=0)
        span = 1
        while span < win:
            s = s[span * BATCH:] + s[:s.shape[0] - span * BATCH]
            span *= 2
        s = s[s.shape[0] - TILE_ROWS:]
        pooled = s / jnp.minimum(n_seen, float(win)) - cur
        ya = jnp.dot(pooled.astype(BF16), w["pmix"][gi], preferred_element_type=F32)
        ya = ya * w["pscale"][:, cols]
        pieces.append(_rms(ya, w["gpo"][:, cols]).astype(BF16))
    pool_tail = v[TILE_ROWS - POOL_HIST * BATCH:]
    group = jnp.where(is_sample, SAMPLE, PROMPT)
    npool_ref[group] = pool_tail
    vhist[...] = pool_tail

    yb = gb * _causal_conv(chist[...], cu, w["convw"][...])
    conv_tail = cu[TILE_ROWS - CONV_HIST * BATCH:]
    nconv_ref[group] = conv_tail
    chist[...] = conv_tail
    sq = (yb * yb).astype(BF16)
    ri = lax.broadcasted_iota(jnp.int32, (SEG_BLOCK, SEG_BLOCK), 0) // CONV_HEAD
    ci = lax.broadcasted_iota(jnp.int32, (SEG_BLOCK, SEG_BLOCK), 1) // CONV_HEAD
    seg = jnp.where(ri == ci, 1.0 / CONV_HEAD, 0.0).astype(BF16)
    for j in range(D_CONV // SEG_BLOCK):
        cols = slice(j * SEG_BLOCK, (j + 1) * SEG_BLOCK)
        ms = jnp.dot(sq[:, cols], seg, preferred_element_type=F32)
        ybn = yb[:, cols] * lax.rsqrt(ms + EPS) * w["gco"][:, cols]
        pieces.append(ybn.astype(BF16))

    cat = jnp.concatenate(pieces, axis=-1)
    project(t + 1, 0)
    project(t + 1, 1)
    mix = jnp.dot(cat, w["w_out"][...], preferred_element_type=F32)

    x1 = _tile_rows(x_ref) + _rms(mix, w["g2"][...])
    h2 = _unit_rms(x1).astype(BF16)
    project(t + 1, 2)

    acts = []
    for c in range(N_FF_CHUNKS):
        halves = []
        for lo in (c * FF_CHUNK, D_FF + c * FF_CHUNK):
            cols = slice(lo, lo + FF_CHUNK)
            up = jnp.dot(h2, w["w_up"][:, cols], preferred_element_type=F32)
            halves.append(_causal_conv(fhist[:, cols], up, w["fcw"][:, cols]))
            tail = up[TILE_ROWS - CONV_HIST * BATCH:]
            nffn_ref[group, :, cols] = tail
            fhist[:, cols] = tail
        gate, val = halves
        acts.append((gate / (1.0 + jnp.exp(-gate)) * val).astype(BF16))
    act = jnp.concatenate(acts, axis=-1)
    f = jnp.dot(act, w["w_down"][...], preferred_element_type=F32)
    out = x1 + _rms(f, w["g4"][...])
    if final:
        out = _rms(out, w["gf"][...])
    project(t + 1, 3)
    obuf[lax.rem(t, OUT_SLOTS)] = out.reshape(TILE_POS, BATCH, D_MODEL)

    store_start(t)

    @pl.when(is_sample)
    def _():
        store_wait(t - 1)
        store_wait(t)


def _layer_call(xp, xs, states, weights, layer, *, final):
    n_prompt_tiles = xp.shape[1] // TILE_POS
    assert n_prompt_tiles >= X_SLOTS
    kern = functools.partial(_layer_kernel, n_prompt_tiles=n_prompt_tiles, layer=layer,
                             final=final)

    def state_spec(a):
        return pl.BlockSpec((None,) + a.shape[1:], lambda t: (layer, 0, 0))

    def out_state_spec(a):
        return pl.BlockSpec((SAMPLE + 1,) + a.shape[1:], lambda t: (0, 0, 0))

    def weight_spec(a):
        if a.ndim == 2:
            return pl.BlockSpec(a.shape, lambda t: (0, 0), pipeline_mode=pl.Buffered(1))
        zeros = (0,) * (a.ndim - 1)
        return pl.BlockSpec((None,) + a.shape[1:], lambda t: (layer,) + zeros,
                            pipeline_mode=pl.Buffered(1))

    any_spec = pl.BlockSpec(memory_space=pl.ANY)
    named = dict(zip(_WEIGHT_NAMES, weights))
    weight_specs = [any_spec if n in MATMUL_WEIGHTS else weight_spec(named[n])
                    for n in _WEIGHT_NAMES]
    return pl.pallas_call(
        kern,
        out_shape=(jax.ShapeDtypeStruct(xp.shape, F32), jax.ShapeDtypeStruct(xs.shape, F32))
        + tuple(jax.ShapeDtypeStruct((SAMPLE + 1,) + s.shape[1:], F32) for s in states),
        grid=(n_prompt_tiles + 1,),
        in_specs=[any_spec, any_spec] + [state_spec(s) for s in states]
        + weight_specs,
        out_specs=(any_spec, any_spec) + tuple(out_state_spec(s) for s in states),
        scratch_shapes=[pltpu.VMEM(s.shape[1:], F32) for s in states] + [
            pltpu.VMEM((X_SLOTS, TILE_POS, BATCH, D_MODEL), F32),
            pltpu.VMEM((OUT_SLOTS, TILE_POS, BATCH, D_MODEL), F32),
            pltpu.VMEM((TILE_ROWS, D_MODEL), BF16),
            pltpu.VMEM((Z_SLOTS, TILE_ROWS, D_IN), F32),
            pltpu.SemaphoreType.DMA((X_SLOTS,)),
            pltpu.SemaphoreType.DMA((OUT_SLOTS,)),
        ] + [pltpu.VMEM(named[n].shape[1:], BF16) for n in MATMUL_WEIGHTS],
        compiler_params=pltpu.CompilerParams(
            dimension_semantics=("arbitrary",),
            vmem_limit_bytes=VMEM_LIMIT_BYTES),
        name="encoder_layer",
    )(xp, xs, *states, *weights)


def _batch_major(a, hist):
    a = a.reshape(a.shape[:-2] + (hist, BATCH, a.shape[-1]))
    return jnp.swapaxes(a, -3, -2)


def _state_in(s, hist):
    s = jnp.swapaxes(s, 1, 2)
    s = jnp.pad(s, ((0, 0), (hist - s.shape[1], 0), (0, 0), (0, 0)))
    return s.reshape(s.shape[0], hist * BATCH, s.shape[-1])


def kernel(x_prompt, x_sample, state_pool, state_conv, state_ffn_conv, w_in, pool_mix,
           pool_scale, conv_w, g_pool_out, g_conv_out, w_out, g_pre_mix, g_post_mix,
           g_pre_ffn, g_post_ffn, w_up, ffn_conv_w, w_down, g_final):
    depth = w_in.shape[0]
    bp, sp, _ = x_prompt.shape
    bs, ss, _ = x_sample.shape
    assert bp == BATCH and bs == BATCH and sp % TILE_POS == 0 and ss == TILE_POS

    row = lambda a: a[:, None, :]
    col = lambda a: a[:, :, None]
    weights = dict(
        w_in=w_in, pmix=pool_mix.astype(BF16), pscale=row(pool_scale), convw=conv_w,
        gpo=row(g_pool_out), gco=row(g_conv_out), w_out=w_out, g1=col(g_pre_mix),
        g2=row(g_post_mix), g3=col(g_pre_ffn), g4=row(g_post_ffn), w_up=w_up,
        fcw=ffn_conv_w, w_down=w_down, gf=g_final[None, :])
    weights = tuple(weights[n] for n in _WEIGHT_NAMES)
    states = (_state_in(state_pool, POOL_HIST), _state_in(state_conv, CONV_HIST),
              _state_in(state_ffn_conv, CONV_HIST))

    xp, xs = x_prompt, x_sample
    pools, convs, ffns = [], [], []
    for l in range(depth):
        xp, xs, a, c, f = _layer_call(xp, xs, states, weights, l, final=(l == depth - 1))
        pools.append(a)
        convs.append(c)
        ffns.append(f)

    pools = _batch_major(jnp.stack(pools), POOL_HIST)[..., POOL_HIST - POOL_STATE:, :]
    convs = _batch_major(jnp.stack(convs), CONV_HIST)
    ffns = _batch_major(jnp.stack(ffns), CONV_HIST)
    return (xp, xs, pools[:, PROMPT], convs[:, PROMPT], ffns[:, PROMPT],
            pools[:, SAMPLE], convs[:, SAMPLE], ffns[:, SAMPLE])
```
